```python
import math
import jax
import jax.numpy as jnp
from jax import lax
import numpy as np

D_MODEL = 1024
BATCH = 16
SEQ = 2048
DEPTH = 2
DEC_BATCH = 32
DEC_SEQ = 4
PAST_LEN = 16384
PAGE_SIZE = 128

HEAD_DIM = 64
NSA_HEADS = 8
NSA_KV_HEADS = 2
NSA_REP = NSA_HEADS // NSA_KV_HEADS
CMP_BLOCK = 32
SEL_BLOCK = 64
N_SELECT = 16
WINDOW = 512
CMP_HIDDEN = HEAD_DIM
SEL_QBLK = 64
WIN_QBLK = 128
ROT_DIM = HEAD_DIM // 4
ROPE_THETA = 500000.0
GLA_HEADS = 4
GLA_DK = 32
GLA_DV = 64
GLA_RANK = 16
GLA_TAU = 16.0
GLA_CHUNK = 16
RET_HEADS = 4
RET_DIM = 64
RET_CHUNK = 64
RET_THETA = 10000.0
MEM_TOKENS = 256
MEM_HEADS = 4
MEM_HD = D_MODEL // MEM_HEADS
D_FF = 4 * D_MODEL
EPS = 1e-6
MIX_W = NSA_HEADS * HEAD_DIM + GLA_HEADS * GLA_DV + RET_HEADS * RET_DIM
IN_SIZES = (NSA_HEADS * HEAD_DIM, 6 * NSA_KV_HEADS * HEAD_DIM, 3 * NSA_HEADS,
            GLA_HEADS * GLA_DK, GLA_HEADS * GLA_DK, GLA_HEADS * GLA_DV, GLA_RANK, GLA_HEADS * GLA_DV,
            RET_HEADS * RET_DIM, RET_HEADS * RET_DIM, RET_HEADS * RET_DIM, RET_HEADS * RET_DIM)
N_IN = sum(IN_SIZES)

kernel_name = 'hybrid_nsa_gla_retnet_decode_step'


def _rmsnorm(x, gain=None):
    xf = x.astype(jnp.float32)
    y = xf * lax.rsqrt(jnp.mean(xf * xf, axis=-1, keepdims=True) + EPS)
    if gain is not None:
        y = y * gain.astype(jnp.float32)
    return y.astype(x.dtype)


def _rope(x, pos, inv_freq):
    nf = inv_freq.shape[0]
    ang = pos.astype(jnp.float32)[:, None] * inv_freq[None, :]
    bshape = (pos.shape[0],) + (1,) * (x.ndim - 3) + (nf,)
    cos = jnp.cos(ang).reshape(bshape).astype(x.dtype)
    sin = jnp.sin(ang).reshape(bshape).astype(x.dtype)
    x1, x2, rest = x[..., :nf], x[..., nf:2 * nf], x[..., 2 * nf:]
    return jnp.concatenate([x1 * cos - x2 * sin, x1 * sin + x2 * cos, rest], axis=-1)


def _nsa_inv_freq():
    return ROPE_THETA ** (-jnp.arange(0, ROT_DIM, 2, dtype=jnp.float32) / ROT_DIM)


def _ret_inv_freq():
    return RET_THETA ** (-jnp.linspace(0.0, 1.0, RET_DIM // 2, dtype=jnp.float32))


def _ret_log_decay():
    return jnp.log(1.0 - 2.0 ** (-5.0 - jnp.arange(RET_HEADS, dtype=jnp.float32)))


def _masked_softmax(s, mask):
    s = jnp.where(mask, s, -jnp.inf)
    m = jnp.max(s, axis=-1, keepdims=True)
    m = jnp.where(jnp.isfinite(m), m, 0.0)
    e = jnp.exp(s - m)
    d = jnp.sum(e, axis=-1, keepdims=True)
    return e / jnp.where(d > 0, d, 1.0)


def _compress(rows, pe, w1, w2):
    b, l, g, d = rows.shape
    nb = l // CMP_BLOCK
    blk = rows.reshape(b, nb, CMP_BLOCK, g, d) + pe[None, None, :, None, :]
    flat = jnp.swapaxes(blk, 2, 3).reshape(b, nb, g, CMP_BLOCK * d)
    return jax.nn.gelu(flat @ w1) @ w2


def _gather_sel(new_rows, idx, past_len, pool=None, layer=0, page_table=None):
    b_, ln, _, g_, _ = new_rows.shape
    bi = jnp.arange(b_)[:, None, None, None, None]
    gi = jnp.arange(g_)[None, :, None, None, None]
    tok = idx[..., None] * SEL_BLOCK + jnp.arange(SEL_BLOCK)
    kv = new_rows[bi, jnp.clip(tok - past_len, 0, ln - 1), :, gi]
    if pool is not None:
        n_pool = pool.shape[1]
        flat_pool = pool.reshape((-1,) + pool.shape[2:])
        tp = jnp.minimum(tok, past_len - 1)
        phys = layer * n_pool + page_table[bi, tp // PAGE_SIZE]
        kv_past = flat_pool[phys, tp % PAGE_SIZE, :, gi]
        kv = jnp.where((tok < past_len)[..., None, None], kv_past, kv)
    return kv[..., 0, :], kv[..., 1, :]


def _nsa_cmp_sel(qg, q_pos, cmp_rows, fetch, p):
    b, t, g, r, d = qg.shape
    scale = d ** -0.5
    kc = _compress(cmp_rows[:, :, 0], p['cmp_pe'][0], p['cmp_w1'][0], p['cmp_w2'][0])
    vc = _compress(cmp_rows[:, :, 1], p['cmp_pe'][1], p['cmp_w1'][1], p['cmp_w2'][1])
    nb = kc.shape[1]
    s = jnp.einsum('btgrd,bngd->bgrtn', qg, kc).astype(jnp.float32) * scale
    blk_end = (jnp.arange(nb) + 1) * CMP_BLOCK - 1
    pc = _masked_softmax(s, blk_end[None, :] <= q_pos[:, None])
    o_cmp = jnp.einsum('bgrtn,bngd->btgrd', pc.astype(vc.dtype), vc)
    nsb = nb * CMP_BLOCK // SEL_BLOCK
    imp = pc.reshape(b, g, r, t, nsb, SEL_BLOCK // CMP_BLOCK).sum(axis=(2, 5))
    j = jnp.arange(nsb)[None, :]
    cur = (q_pos // SEL_BLOCK)[:, None]
    forced = (j == 0) | (j == cur) | (j == cur - 1)
    valid = j * SEL_BLOCK <= q_pos[:, None]
    score = jnp.where(forced, jnp.inf, jnp.where(valid, imp, -jnp.inf))
    _, idx = lax.top_k(score, min(N_SELECT, nsb))
    qb = math.gcd(t, SEL_QBLK)
    nq = t // qb
    q_blocks = jnp.moveaxis(qg.reshape(b, nq, qb, g, r, d), 1, 0)
    i_blocks = jnp.moveaxis(idx.reshape(b, g, nq, qb, idx.shape[-1]), 2, 0)
    t_blocks = q_pos.reshape(nq, qb)

    def one(args):
        qq, ii, tt = args
        kk, vv = fetch(ii)
        kpos = ii[..., None] * SEL_BLOCK + jnp.arange(SEL_BLOCK)
        ss = jnp.einsum('bqgrd,bgqnkd->bgrqnk', qq, kk).astype(jnp.float32) * scale
        ss = jnp.where((kpos <= tt[None, None, :, None, None])[:, :, None], ss, -jnp.inf)
        pp = jax.nn.softmax(ss.reshape(ss.shape[:4] + (-1,)), axis=-1).astype(vv.dtype)
        return jnp.einsum('bgrqm,bgqmd->bqgrd', pp, vv.reshape(vv.shape[:3] + (-1, d)))

    o_sel = lax.map(one, (q_blocks, i_blocks, t_blocks))
    o_sel = jnp.moveaxis(o_sel, 0, 1).reshape(b, t, g, r, d)
    return o_cmp, o_sel


def _window_attend(qg, q_pos, k, v, k_pos):
    scale = qg.shape[-1] ** -0.5
    s = jnp.einsum('bqgrd,bkgd->bgrqk', qg, k).astype(jnp.float32) * scale
    kp = k_pos[None, :]
    qp = q_pos[:, None]
    mask = (kp <= qp) & (kp > qp - WINDOW) & (kp >= 0)
    pw = jax.nn.softmax(jnp.where(mask, s, -jnp.inf), axis=-1).astype(v.dtype)
    return jnp.einsum('bgrqk,bkgd->bqgrd', pw, v)


def _window_prompt(qg, k, v):
    b, t = qg.shape[:2]
    qb = math.gcd(t, WIN_QBLK)
    span = WINDOW + qb
    padw = ((0, 0), (WINDOW, 0), (0, 0), (0, 0))
    kp, vp = jnp.pad(k, padw), jnp.pad(v, padw)

    def one(i):
        start = i * qb
        qq = lax.dynamic_slice_in_dim(qg, start, qb, axis=1)
        kk = lax.dynamic_slice_in_dim(kp, start, span, axis=1)
        vv = lax.dynamic_slice_in_dim(vp, start, span, axis=1)
        return _window_attend(qq, start + jnp.arange(qb), kk, vv, start - WINDOW + jnp.arange(span))

    o = lax.map(one, jnp.arange(t // qb))
    return jnp.moveaxis(o, 0, 1).reshape(qg.shape)


def _to_chunks(a, chunk):
    b, h, t, d = a.shape
    return jnp.moveaxis(a.reshape(b, h, t // chunk, chunk, d), 2, 0)


def _from_chunks(o):
    n, b, h, c, d = o.shape
    return jnp.moveaxis(o, 0, 2).reshape(b, h, n * c, d)


def _gla(q, k, v, log_a, s0, chunk):
    f32 = jnp.float32
    causal = jnp.tril(jnp.ones((chunk, chunk), bool))[None, None, :, :, None]

    def step(S, inp):
        qc, kc, vc, ac = (a.astype(f32) for a in inp)
        cb = jnp.cumsum(ac, axis=2)
        decay = jnp.exp(jnp.where(causal, cb[:, :, :, None, :] - cb[:, :, None, :, :], -jnp.inf))
        attn = jnp.einsum('bhtd,bhsd,bhtsd->bhts', qc, kc, decay)
        o = jnp.einsum('bhtd,bhde->bhte', qc * jnp.exp(cb), S) + jnp.einsum('bhts,bhse->bhte', attn, vc)
        last = cb[:, :, -1:, :]
        S = jnp.exp(last)[:, :, 0, :, None] * S + jnp.einsum('bhsd,bhse->bhde', kc * jnp.exp(last - cb), vc)
        return S, o.astype(v.dtype)

    S, o = lax.scan(step, s0.astype(f32), tuple(_to_chunks(a, chunk) for a in (q, k, v, log_a)))
    return _from_chunks(o), S.astype(s0.dtype)


def _retention(q, k, v, s0, chunk):
    f32 = jnp.float32
    lg = _ret_log_decay()[:, None]
    i = jnp.arange(chunk, dtype=f32)
    rel = i[:, None] - i[None, :]
    dmat = jnp.where(rel >= 0, jnp.exp(lg[:, :, None] * jnp.maximum(rel, 0.0)), 0.0)
    inner = jnp.exp(lg * (i + 1.0))[None, :, :, None]
    tail = jnp.exp(lg * (chunk - 1.0 - i))[None, :, :, None]
    g_chunk = jnp.exp(lg * chunk)[None, :, :, None]

    def step(S, inp):
        qc, kc, vc = (a.astype(f32) for a in inp)
        attn = jnp.einsum('bhtd,bhsd->bhts', qc, kc) * dmat
        o = jnp.einsum('bhts,bhse->bhte', attn, vc) + jnp.einsum('bhtd,bhde->bhte', qc, S) * inner
        S = g_chunk * S + jnp.einsum('bhsd,bhse->bhde', kc * tail, vc)
        return S, o.astype(v.dtype)

    S, o = lax.scan(step, s0.astype(f32), tuple(_to_chunks(a, chunk) for a in (q, k, v)))
    return _from_chunks(o), S.astype(s0.dtype)


def _project(h, pos, p):
    b, t, _ = h.shape
    parts = jnp.split(h @ p['w_in'], np.cumsum(IN_SIZES)[:-1].tolist(), axis=-1)
    nq, nkv, ngate, gq, gk, gv, ga, gr, rq, rk, rv, rg = parts
    nsa_q = _rope(nq.reshape(b, t, NSA_HEADS, HEAD_DIM), pos, _nsa_inv_freq())
    kv = nkv.reshape(b, t, 3, 2, NSA_KV_HEADS, HEAD_DIM)
    rows = jnp.stack([_rope(kv[:, :, :, 0], pos, _nsa_inv_freq()), kv[:, :, :, 1]], axis=3)

    def heads(z, n, d):
        return jnp.swapaxes(z.reshape(b, t, n, d), 1, 2)

    log_a = jax.nn.log_sigmoid((ga @ p['gla_wa2'] + p['gla_ba']).astype(jnp.float32)) / GLA_TAU
    ret_f = _ret_inv_freq()
    ret_q = jnp.swapaxes(_rope(rq.reshape(b, t, RET_HEADS, RET_DIM), pos, ret_f), 1, 2)
    ret_k = jnp.swapaxes(_rope(rk.reshape(b, t, RET_HEADS, RET_DIM), pos, ret_f), 1, 2) * RET_DIM ** -0.5
    return {
        'q': nsa_q.reshape(b, t, NSA_KV_HEADS, NSA_REP, HEAD_DIM),
        'cmp': rows[:, :, 0], 'slc': rows[:, :, 1], 'win': rows[:, :, 2],
        'gates': jax.nn.sigmoid(ngate.reshape(b, t, NSA_HEADS, 3)),
        'gla_q': heads(gq * GLA_DK ** -0.5, GLA_HEADS, GLA_DK),
        'gla_k': heads(gk, GLA_HEADS, GLA_DK),
        'gla_v': heads(gv, GLA_HEADS, GLA_DV),
        'gla_a': heads(log_a, GLA_HEADS, GLA_DK),
        'gla_r': gr,
        'ret_q': ret_q, 'ret_k': ret_k, 'ret_v': heads(rv, RET_HEADS, RET_DIM), 'ret_g': rg,
    }


def _mix_out(pr, o_cmp, o_sel, o_win, o_gla, o_ret, p):
    b, t = o_cmp.shape[:2]
    g = pr['gates']
    shp = (b, t, NSA_HEADS, HEAD_DIM)
    o_nsa = o_cmp.reshape(shp) * g[..., 0:1] + o_sel.reshape(shp) * g[..., 1:2] + o_win.reshape(shp) * g[..., 2:3]
    o_gla = _rmsnorm(jnp.swapaxes(o_gla, 1, 2), p['gla_norm']).reshape(b, t, -1) * jax.nn.silu(pr['gla_r'])
    o_ret = _rmsnorm(jnp.swapaxes(o_ret, 1, 2)).reshape(b, t, -1) * jax.nn.silu(pr['ret_g'])
    return jnp.concatenate([o_nsa.reshape(b, t, -1), o_gla, o_ret], axis=-1) @ p['w_out']


def _cross(h, mem_kv, p):
    b, t, _ = h.shape
    q = (h @ p['w_mq']).reshape(b, t, MEM_HEADS, MEM_HD)
    s = jnp.einsum('bqhd,bkhd->bhqk', q, mem_kv[:, :, 0]).astype(jnp.float32) * MEM_HD ** -0.5
    pm = jax.nn.softmax(s, axis=-1).astype(h.dtype)
    return jnp.einsum('bhqk,bkhd->bqhd', pm, mem_kv[:, :, 1]).reshape(b, t, -1) @ p['w_mo']


def _ffn(h, p):
    return jnp.square(jax.nn.relu(h @ p['w_ff1'])) @ p['w_ff2']


def _prompt_layer(x, mem, p):
    b, t, _ = x.shape
    pos = jnp.arange(t)
    pr = _project(_rmsnorm(x, p['norm_mix']), pos, p)
    slc = pr['slc']
    o_cmp, o_sel = _nsa_cmp_sel(pr['q'], pos, pr['cmp'], lambda idx: _gather_sel(slc, idx, 0), p)
    o_win = _window_prompt(pr['q'], pr['win'][:, :, 0], pr['win'][:, :, 1])
    o_gla, s_gla = _gla(pr['gla_q'], pr['gla_k'], pr['gla_v'], pr['gla_a'],
                        jnp.zeros((b, GLA_HEADS, GLA_DK, GLA_DV), x.dtype), math.gcd(t, GLA_CHUNK))
    o_ret, s_ret = _retention(pr['ret_q'], pr['ret_k'], pr['ret_v'],
                              jnp.zeros((b, RET_HEADS, RET_DIM, RET_DIM), x.dtype), math.gcd(t, RET_CHUNK))
    x = x + _mix_out(pr, o_cmp, o_sel, o_win, o_gla, o_ret, p)
    mem_kv = (mem @ p['w_mkv']).reshape(b, mem.shape[1], 2, MEM_HEADS, MEM_HD)
    x = x + _cross(_rmsnorm(x, p['norm_mem']), mem_kv, p)
    x = x + _ffn(_rmsnorm(x, p['norm_ffn']), p)
    return x, (pr['cmp'], slc, pr['win'][:, t - min(WINDOW, t):], s_gla, s_ret, mem_kv)


def _sample_layer(x, l, cache_cmp_kv, cache_slc_kv, cache_win_kv, state_gla, state_ret, cache_mem_kv, page_table, p):
    b, t, _ = x.shape
    past = page_table.shape[1] * PAGE_SIZE
    pos = past + jnp.arange(t)
    pr = _project(_rmsnorm(x, p['norm_mix']), pos, p)
    padw = ((0, 0), (0, (-t) % SEL_BLOCK), (0, 0), (0, 0), (0, 0))
    n_pool = cache_cmp_kv.shape[1]
    pool_cmp = cache_cmp_kv.reshape((-1,) + cache_cmp_kv.shape[2:])
    past_cmp = pool_cmp[l * n_pool + page_table].reshape((b, past) + cache_cmp_kv.shape[3:])
    cmp_rows = jnp.concatenate([past_cmp, jnp.pad(pr['cmp'], padw)], axis=1)
    slc_new = jnp.pad(pr['slc'], padw)
    fetch = lambda idx: _gather_sel(slc_new, idx, past, cache_slc_kv, l, page_table)
    o_cmp, o_sel = _nsa_cmp_sel(pr['q'], pos, cmp_rows, fetch, p)
    buf = cache_win_kv[l]
    wb = buf.shape[1]
    win_rows = jnp.concatenate([buf, pr['win']], axis=1)
    o_win = _window_attend(pr['q'], pos, win_rows[:, :, 0], win_rows[:, :, 1], past - wb + jnp.arange(wb + t))
    o_gla, s_gla = _gla(pr['gla_q'], pr['gla_k'], pr['gla_v'], pr['gla_a'], state_gla[l], math.gcd(t, GLA_CHUNK))
    o_ret, s_ret = _retention(pr['ret_q'], pr['ret_k'], pr['ret_v'], state_ret[l], math.gcd(t, RET_CHUNK))
    x = x + _mix_out(pr, o_cmp, o_sel, o_win, o_gla, o_ret, p)
    x = x + _cross(_rmsnorm(x, p['norm_mem']), cache_mem_kv[l], p)
    x = x + _ffn(_rmsnorm(x, p['norm_ffn']), p)
    return x, (pr['cmp'], pr['slc'], win_rows[:, t:], s_gla, s_ret)


def setup_inputs(seed: int = 0) -> dict:
    key = jax.random.key(seed)
    ks = iter(jax.random.split(key, 40))

    def nrm(shape, scale=1.0):
        return jax.random.normal(next(ks), shape, jnp.float32) * scale

    def gain(shape):
        return 1.0 + nrm(shape, 0.01)

    n_pages = PAST_LEN // PAGE_SIZE
    n_pool = (5 * DEC_BATCH * n_pages + 3) // 4
    kv_row = (2, NSA_KV_HEADS, HEAD_DIM)
    page_table = jax.random.permutation(next(ks), n_pool)[:DEC_BATCH * n_pages].reshape(DEC_BATCH, n_pages).astype(jnp.int32)
    return {
        'x_prompt': nrm((BATCH, SEQ, D_MODEL)),
        'x_sample': nrm((DEC_BATCH, DEC_SEQ, D_MODEL)),
        'mem_prompt': nrm((BATCH, MEM_TOKENS, D_MODEL)),
        'cache_cmp_kv': nrm((DEPTH, n_pool, PAGE_SIZE) + kv_row),
        'cache_slc_kv': nrm((DEPTH, n_pool, PAGE_SIZE) + kv_row),
        'cache_win_kv': nrm((DEPTH, DEC_BATCH, min(WINDOW, PAST_LEN)) + kv_row),
        'state_gla': nrm((DEPTH, DEC_BATCH, GLA_HEADS, GLA_DK, GLA_DV)),
        'state_ret': nrm((DEPTH, DEC_BATCH, RET_HEADS, RET_DIM, RET_DIM)),
        'cache_mem_kv': nrm((DEPTH, DEC_BATCH, MEM_TOKENS, 2, MEM_HEADS, MEM_HD)),
        'page_table': page_table,
        'norm_mix': gain((DEPTH, D_MODEL)),
        'w_in': nrm((DEPTH, D_MODEL, N_IN), D_MODEL ** -0.5),
        'gla_wa2': nrm((DEPTH, GLA_RANK, GLA_HEADS * GLA_DK), GLA_RANK ** -0.5),
        'gla_ba': nrm((DEPTH, GLA_HEADS * GLA_DK), 0.01),
        'gla_norm': gain((DEPTH, GLA_DV)),
        'cmp_pe': nrm((DEPTH, 2, CMP_BLOCK, HEAD_DIM), 0.1),
        'cmp_w1': nrm((DEPTH, 2, CMP_BLOCK * HEAD_DIM, CMP_HIDDEN), (CMP_BLOCK * HEAD_DIM) ** -0.5),
        'cmp_w2': nrm((DEPTH, 2, CMP_HIDDEN, HEAD_DIM), CMP_HIDDEN ** -0.5),
        'w_out': nrm((DEPTH, MIX_W, D_MODEL), MIX_W ** -0.5),
        'norm_mem': gain((DEPTH, D_MODEL)),
        'w_mq': nrm((DEPTH, D_MODEL, MEM_HEADS * MEM_HD), D_MODEL ** -0.5),
        'w_mkv': nrm((DEPTH, D_MODEL, 2 * MEM_HEADS * MEM_HD), D_MODEL ** -0.5),
        'w_mo': nrm((DEPTH, MEM_HEADS * MEM_HD, D_MODEL), (MEM_HEADS * MEM_HD) ** -0.5),
        'norm_ffn': gain((DEPTH, D_MODEL)),
        'w_ff1': nrm((DEPTH, D_MODEL, D_FF), D_MODEL ** -0.5),
        'w_ff2': nrm((DEPTH, D_FF, D_MODEL), D_FF ** -0.5),
        'norm_final': gain((D_MODEL,)),
    }


def reference(x_prompt, x_sample, mem_prompt, cache_cmp_kv, cache_slc_kv, cache_win_kv, state_gla, state_ret,
              cache_mem_kv, page_table, norm_mix, w_in, gla_wa2, gla_ba, gla_norm, cmp_pe, cmp_w1, cmp_w2, w_out,
              norm_mem, w_mq, w_mkv, w_mo, norm_ffn, w_ff1, w_ff2, norm_final):
    hp, hs = x_prompt, x_sample
    new_p = [[] for _ in range(6)]
    new_s = [[] for _ in range(5)]
    for l in range(DEPTH):
        p = {'norm_mix': norm_mix[l], 'w_in': w_in[l], 'gla_wa2': gla_wa2[l], 'gla_ba': gla_ba[l],
             'gla_norm': gla_norm[l], 'cmp_pe': cmp_pe[l], 'cmp_w1': cmp_w1[l], 'cmp_w2': cmp_w2[l],
             'w_out': w_out[l], 'norm_mem': norm_mem[l], 'w_mq': w_mq[l], 'w_mkv': w_mkv[l], 'w_mo': w_mo[l],
             'norm_ffn': norm_ffn[l], 'w_ff1': w_ff1[l], 'w_ff2': w_ff2[l]}
        hp, st_p = _prompt_layer(hp, mem_prompt, p)
        for lst, a in zip(new_p, st_p):
            lst.append(a)
        hs, st_s = _sample_layer(hs, l, cache_cmp_kv, cache_slc_kv, cache_win_kv, state_gla, state_ret,
                                 cache_mem_kv, page_table, p)
        for lst, a in zip(new_s, st_s):
            lst.append(a)
    y_prompt = _rmsnorm(hp, norm_final)
    y_sample = _rmsnorm(hs, norm_final)
    cmp_p, slc_p, win_p, gla_p, ret_p, mem_p = [jnp.stack(a) for a in new_p]
    cmp_s, slc_s, win_s, gla_s, ret_s = [jnp.stack(a) for a in new_s]
    return (y_prompt, y_sample, cmp_p, slc_p, win_p, gla_p, ret_p, mem_p, cmp_s, slc_s, win_s, gla_s, ret_s)
```

```python
import functools
import math

import numpy as np
import jax
import jax.numpy as jnp
from jax import lax
from jax.experimental import pallas as pl
from jax.experimental.pallas import tpu as pltpu

F32 = jnp.float32
BF16 = jnp.bfloat16

D_MODEL = 1024
PAGE_SIZE = 128
HEAD_DIM = 64
NSA_HEADS = 8
NSA_KV_HEADS = 2
NSA_REP = NSA_HEADS // NSA_KV_HEADS
CMP_BLOCK = 32
SEL_BLOCK = 64
N_SELECT = 16
WINDOW = 512
ROT_DIM = HEAD_DIM // 4
ROPE_THETA = 500000.0
GLA_HEADS = 4
GLA_DK = 32
GLA_DV = 64
GLA_RANK = 16
GLA_TAU = 16.0
RET_HEADS = 4
RET_DIM = 64
RET_THETA = 10000.0
MEM_HEADS = 4
MEM_HD = D_MODEL // MEM_HEADS
D_FF = 4 * D_MODEL
EPS = 1e-6
IN_SIZES = (NSA_HEADS * HEAD_DIM, 6 * NSA_KV_HEADS * HEAD_DIM, 3 * NSA_HEADS,
            GLA_HEADS * GLA_DK, GLA_HEADS * GLA_DK, GLA_HEADS * GLA_DV, GLA_RANK, GLA_HEADS * GLA_DV,
            RET_HEADS * RET_DIM, RET_HEADS * RET_DIM, RET_HEADS * RET_DIM, RET_HEADS * RET_DIM)

LANES = 128
VMEM_LIMIT = 56 << 20
NEG = -1e30
KV_W = 2 * NSA_KV_HEADS * HEAD_DIM

C_Q = 0
C_KV = 512
C_GLA = 1280
C_RET = 2048
C_TAIL = 3072
W_IN_COLS = 3200
GLA_W = 896
RET_W = 1024


def _cparams(sem, vmem=VMEM_LIMIT):
    return pltpu.CompilerParams(dimension_semantics=sem, vmem_limit_bytes=vmem)


def _dot(a, b):
    return jnp.dot(a, b, preferred_element_type=F32)


def _dot_nt(a, b):
    return lax.dot_general(a, b, (((1,), (1,)), ((), ())), preferred_element_type=F32)


def _dot_tn(a, b):
    return lax.dot_general(a, b, (((0,), (0,)), ((), ())), preferred_element_type=F32)


def _split3(x):
    hi = x.astype(BF16)
    r = x - hi.astype(F32)
    mid = r.astype(BF16)
    lo = (r - mid.astype(F32)).astype(BF16)
    return hi, mid, lo


def _rms(x, eps=EPS):
    return x * lax.rsqrt(jnp.mean(x * x, axis=-1, keepdims=True) + eps)


def _silu(x):
    return x * jax.nn.sigmoid(x)


def _rope128(v, c, sa, sb, half):
    return v * c + pltpu.roll(v, LANES - half, 1) * sa + pltpu.roll(v, half, 1) * sb


def _proj_kernel(x_ref, gain_ref, w_ref, wa2_ref, ba_ref, nc_ref, nsa_ref, nsb_ref, rc_ref, rsa_ref, rsb_ref,
                 q_ref, cmp_ref, slc_ref, win_ref, gate_ref, gla_ref, ret_ref, *, valid_period):
    x = x_ref[...]
    xn = _rms(x) * gain_ref[...]
    if valid_period is not None:
        period, n_valid = valid_period
        row = lax.broadcasted_iota(jnp.int32, (x.shape[0], 1), 0)
        xn = jnp.where(lax.rem(row, period) < n_valid, xn, 0.0)
    xb = xn.astype(BF16)

    def mm(a, b):
        return _dot(xb, w_ref[:, a:b])

    nc, nsa, nsb = nc_ref[...], nsa_ref[...], nsb_ref[...]
    rc, rsa, rsb = rc_ref[...], rsa_ref[...], rsb_ref[...]
    half_n = ROT_DIM // 2
    half_r = RET_DIM // 2

    q = mm(C_Q, C_KV)
    for c in range(4):
        sl = slice(c * LANES, (c + 1) * LANES)
        q_ref[:, sl] = (_rope128(q[:, sl], nc, nsa, nsb, half_n) * (HEAD_DIM ** -0.5)).astype(q_ref.dtype)

    kv = mm(C_KV, C_GLA)
    for br, ref in enumerate((cmp_ref, slc_ref, win_ref)):
        ref[:, 0:LANES] = _rope128(kv[:, br * KV_W:br * KV_W + LANES], nc, nsa, nsb, half_n)
        ref[:, LANES:KV_W] = kv[:, br * KV_W + LANES:(br + 1) * KV_W]

    tail = mm(C_TAIL, W_IN_COLS)
    gate_ref[...] = jax.nn.sigmoid(tail)
    z = jnp.dot(tail, wa2_ref[...], preferred_element_type=F32, precision=lax.Precision.HIGHEST) + ba_ref[...]
    log_a = (jnp.minimum(z, 0.0) - jnp.log1p(jnp.exp(-jnp.abs(z)))) * (1.0 / GLA_TAU)

    gl = mm(C_GLA, C_RET)
    gla_ref[:, 0:128] = gl[:, 0:128] * (GLA_DK ** -0.5)
    gla_ref[:, 128:256] = gl[:, 128:256]
    gla_ref[:, 256:384] = log_a
    gla_ref[:, 384:640] = gl[:, 256:512]
    gla_ref[:, 640:896] = _silu(gl[:, 512:768])

    rt = mm(C_RET, C_TAIL)
    for c in range(2):
        sl = slice(c * LANES, (c + 1) * LANES)
        ret_ref[:, sl] = _rope128(rt[:, sl], rc, rsa, rsb, half_r)
        sk = slice(256 + c * LANES, 256 + (c + 1) * LANES)
        ret_ref[:, sk] = _rope128(rt[:, sk], rc, rsa, rsb, half_r) * (RET_DIM ** -0.5)
    ret_ref[:, 512:768] = rt[:, 512:768]
    ret_ref[:, 768:1024] = _silu(rt[:, 768:1024])


def _project(x, gain, w, wa2, ba, tabs, *, tm, valid_period=None):
    n = x.shape[0]
    p = tabs[0].shape[0]
    nt = p // tm
    row = lambda w_: pl.BlockSpec((tm, w_), lambda i: (i, 0))
    full = lambda a: pl.BlockSpec(a.shape, lambda i: (0, 0))
    tab = pl.BlockSpec((tm, LANES), lambda i: (i % nt, 0))
    out_shape = (jax.ShapeDtypeStruct((n, 512), BF16),
                 jax.ShapeDtypeStruct((n, KV_W), F32), jax.ShapeDtypeStruct((n, KV_W), F32),
                 jax.ShapeDtypeStruct((n, KV_W), F32), jax.ShapeDtypeStruct((n, LANES), F32),
                 jax.ShapeDtypeStruct((n, GLA_W), F32), jax.ShapeDtypeStruct((n, RET_W), F32))
    return pl.pallas_call(
        functools.partial(_proj_kernel, valid_period=valid_period),
        out_shape=out_shape,
        grid=(n // tm,),
        in_specs=[row(D_MODEL), full(gain), full(w), full(wa2), full(ba)] + [tab] * 6,
        out_specs=(row(512), row(KV_W), row(KV_W), row(KV_W), row(LANES), row(GLA_W), row(RET_W)),
        compiler_params=_cparams(("parallel",)),
        name="proj",
    )(x, gain, w, wa2, ba, *tabs)


def _rope_tables(pos, inv_freq):
    nf = inv_freq.shape[0]
    ang = pos.astype(F32)[:, None] * inv_freq[None, :]
    cos, sin = jnp.cos(ang), jnp.sin(ang)
    ones = jnp.ones((pos.shape[0], HEAD_DIM - 2 * nf), F32)
    zeros = jnp.zeros((pos.shape[0], HEAD_DIM - nf), F32)
    c = jnp.concatenate([cos, cos, ones], axis=1)
    sa = jnp.concatenate([-sin, zeros], axis=1)
    sb = jnp.concatenate([zeros[:, :nf], sin, zeros[:, :HEAD_DIM - 2 * nf]], axis=1)
    return tuple(jnp.tile(t, (1, LANES // HEAD_DIM)) for t in (c, sa, sb))


def _all_rope_tables(pos):
    nsa_f = ROPE_THETA ** (-jnp.arange(0, ROT_DIM, 2, dtype=F32) / ROT_DIM)
    ret_f = RET_THETA ** (-jnp.linspace(0.0, 1.0, RET_DIM // 2, dtype=F32))
    return _rope_tables(pos, nsa_f) + _rope_tables(pos, ret_f)


def _pack_w_in(w):
    offs = np.cumsum((0,) + IN_SIZES)
    nq, nkv, ngate, gq, gk, gv, ga, gr, rq, rk, rv, rg = [w[:, offs[i]:offs[i + 1]] for i in range(12)]
    pad = jnp.zeros((w.shape[0], W_IN_COLS - C_TAIL - ngate.shape[1] - ga.shape[1]), w.dtype)
    return jnp.concatenate([nq, nkv, gq, gk, gv, gr, rq, rk, rv, rg, ngate, ga, pad], axis=1).astype(BF16)


def _pack_wa2(wa2):
    top = jnp.zeros((3 * NSA_HEADS, wa2.shape[1]), wa2.dtype)
    bot = jnp.zeros((LANES - 3 * NSA_HEADS - GLA_RANK, wa2.shape[1]), wa2.dtype)
    return jnp.concatenate([top, wa2, bot], axis=0)


def _gelu_tanh(x):
    return 0.5 * x * (1.0 + jnp.tanh(0.7978845608028654 * (x + 0.044715 * (x * x * x))))


def _compress_kernel(x_ref, pe_ref, w1_ref, w2_ref, o_ref):
    xb = (x_ref[...] + pe_ref[...]).astype(BF16)
    h = _gelu_tanh(_dot(xb, w1_ref[...]))
    o_ref[...] = _dot(h.astype(BF16), w2_ref[...])


def _compress(rows_flat, pe_big, w1_big, w2_big, *, tr):
    r, k = rows_flat.shape
    tr = min(tr, r)
    full = lambda a: pl.BlockSpec(a.shape, lambda i: (0, 0))
    return pl.pallas_call(
        _compress_kernel,
        out_shape=jax.ShapeDtypeStruct((r, KV_W), F32),
        grid=(r // tr,),
        in_specs=[pl.BlockSpec((tr, k), lambda i: (i, 0)), full(pe_big), full(w1_big), full(w2_big)],
        out_specs=pl.BlockSpec((tr, KV_W), lambda i: (i, 0)),
        compiler_params=_cparams(("parallel",)),
        name="compress",
    )(rows_flat, pe_big, w1_big, w2_big)


def _pack_compress(pe, w1, w2):
    g, d = NSA_KV_HEADS, HEAD_DIM
    hid = w1.shape[-1]
    pe_big = jnp.broadcast_to(pe.transpose(1, 0, 2)[:, :, None, :], (CMP_BLOCK, 2, g, d)).reshape(1, -1)
    eye = jnp.eye(2 * g, dtype=w1.dtype).reshape(2, g, 2, g)
    w1r = w1.reshape(2, CMP_BLOCK, d, hid)
    w1_big = jnp.einsum('kjdc,kgKG->jkgdKGc', w1r, eye).reshape(CMP_BLOCK * KV_W, 2 * g * hid)
    w2_big = jnp.einsum('kce,kgKG->kgcKGe', w2, eye).reshape(2 * g * hid, KV_W)
    return pe_big, w1_big.astype(BF16), w2_big.astype(BF16)


def _stack_heads(q, g):
    return jnp.concatenate([q[:, (g * NSA_REP + r) * HEAD_DIM:(g * NSA_REP + r + 1) * HEAD_DIM]
                            for r in range(NSA_REP)], axis=0)


def _flash_step(s, mask, v_b, m_ref, l_ref, acc_ref):
    s = jnp.where(mask, s, NEG)
    m_old = m_ref[...]
    m_new = jnp.maximum(m_old, jnp.max(s, axis=-1, keepdims=True))
    alpha = jnp.exp(m_old - m_new)
    p = jnp.where(mask, jnp.exp(s - m_new), 0.0)
    l_ref[...] = alpha * l_ref[...] + jnp.sum(p, axis=-1, keepdims=True)
    acc_ref[...] = alpha * acc_ref[...] + _dot(p.astype(BF16), v_b)
    m_ref[...] = m_new


def _nsa_prompt_kernel(q_ref, kcv_ref, slc_ref, win_ref, gate_ref, exp_ref, o_ref,
                       selx_ref, m_ref, l_ref, acc_ref, *, tq, tk):
    i = pl.program_id(1)
    m_rows = NSA_REP * tq
    nsb = exp_ref.shape[0]
    q = q_ref[...]
    gates = gate_ref[...]
    row = lax.broadcasted_iota(jnp.int32, (m_rows, 1), 0)
    tpos = i * tq + lax.rem(row, tq)
    tq_pos = i * tq + lax.broadcasted_iota(jnp.int32, (tq, 1), 0)

    for g in range(NSA_KV_HEADS):
        qg = _stack_heads(q, g)
        kcol = slice(g * HEAD_DIM, (g + 1) * HEAD_DIM)
        vcol = slice((NSA_KV_HEADS + g) * HEAD_DIM, (NSA_KV_HEADS + g + 1) * HEAD_DIM)

        nb = kcv_ref.shape[0]
        kc = kcv_ref[:, kcol].astype(BF16)
        vc = kcv_ref[:, vcol].astype(BF16)
        s = _dot_nt(qg, kc)
        col = lax.broadcasted_iota(jnp.int32, (1, nb), 1)
        blk = jnp.where(col < nb // 2, 2 * col, 2 * (col - nb // 2) + 1)
        cmask = (blk + 1) * CMP_BLOCK - 1 <= tpos
        sm = jnp.where(cmask, s, NEG)
        mx = jnp.max(sm, axis=-1, keepdims=True)
        mx = jnp.where(mx > 0.5 * NEG, mx, 0.0)
        e = jnp.where(cmask, jnp.exp(sm - mx), 0.0)
        den = jnp.sum(e, axis=-1, keepdims=True)
        p = e / jnp.where(den > 0.0, den, 1.0)
        o_cmp = _dot(p.astype(BF16), vc)
        ps = p[0:tq]
        for r in range(1, NSA_REP):
            ps = ps + p[r * tq:(r + 1) * tq]
        imp = ps[:, 0:nb // 2] + ps[:, nb // 2:nb]

        j = lax.broadcasted_iota(jnp.int32, (1, nsb), 1)
        cur = tq_pos // SEL_BLOCK
        forced = (j == 0) | (j == cur) | (j == cur - 1)
        valid = j * SEL_BLOCK <= tq_pos
        score = jnp.where(forced, -NEG, jnp.where(valid, imp, NEG))
        rank = jnp.zeros((tq, nsb), F32)
        for c in range(nsb):
            sc = score[:, c:c + 1]
            beats = (sc > score) | ((sc == score) & (j > c))
            rank = rank + jnp.where(beats, 1.0, 0.0)
        sel = jnp.where(rank < float(N_SELECT), 1.0, 0.0).astype(BF16)
        selx_ref[...] = _dot(sel, exp_ref[...])

        def init():
            m_ref[...] = jnp.full((m_rows, 1), NEG, F32)
            l_ref[...] = jnp.zeros((m_rows, 1), F32)
            acc_ref[...] = jnp.zeros((m_rows, HEAD_DIM), F32)

        def sel_body(kt, carry):
            k0 = pl.multiple_of(kt * tk, tk)
            kb = slc_ref[pl.ds(k0, tk), kcol].astype(BF16)
            vb = slc_ref[pl.ds(k0, tk), vcol].astype(BF16)
            kpos = k0 + lax.broadcasted_iota(jnp.int32, (1, tk), 1)
            sx = selx_ref[:, pl.ds(k0, tk)]
            sx = jnp.concatenate([sx] * NSA_REP, axis=0)
            mask = (kpos <= tpos) & (sx > 0.5)
            _flash_step(_dot_nt(qg, kb), mask, vb, m_ref, l_ref, acc_ref)
            return carry

        init()
        lax.fori_loop(0, ((i + 1) * tq + tk - 1) // tk, sel_body, 0)
        o_sel = acc_ref[...] / l_ref[...]

        def win_body(kt, carry):
            k0 = pl.multiple_of(kt * tk, tk)
            kb = win_ref[pl.ds(k0, tk), kcol].astype(BF16)
            vb = win_ref[pl.ds(k0, tk), vcol].astype(BF16)
            kpos = k0 + lax.broadcasted_iota(jnp.int32, (1, tk), 1)
            mask = (kpos <= tpos) & (kpos > tpos - WINDOW)
            _flash_step(_dot_nt(qg, kb), mask, vb, m_ref, l_ref, acc_ref)
            return carry

        init()
        lo = jnp.maximum(i * tq - WINDOW + 1, 0) // tk
        lax.fori_loop(lo, ((i + 1) * tq + tk - 1) // tk, win_body, 0)
        o_win = acc_ref[...] / l_ref[...]

        outs = []
        for r in range(NSA_REP):
            h = g * NSA_REP + r
            rs = slice(r * tq, (r + 1) * tq)
            outs.append(o_cmp[rs] * gates[:, 3 * h:3 * h + 1] + o_sel[rs] * gates[:, 3 * h + 1:3 * h + 2]
                        + o_win[rs] * gates[:, 3 * h + 2:3 * h + 3])
        o_ref[:, g * NSA_REP * HEAD_DIM:(g + 1) * NSA_REP * HEAD_DIM] = jnp.concatenate(outs, axis=1)


def _nsa_prompt(q, kcv, slc, win, gates, *, b, t, tq, tk):
    n = b * t
    nq = t // tq
    nb = t // CMP_BLOCK
    nsb = t // SEL_BLOCK
    expand = (np.arange(t)[None, :] // SEL_BLOCK == np.arange(nsb)[:, None]).astype(np.float32)
    expand = jnp.asarray(expand, BF16)
    m_rows = NSA_REP * tq
    return pl.pallas_call(
        functools.partial(_nsa_prompt_kernel, tq=tq, tk=tk),
        out_shape=jax.ShapeDtypeStruct((n, NSA_HEADS * HEAD_DIM), F32),
        grid=(b, nq),
        in_specs=[pl.BlockSpec((tq, 512), lambda bb, i: (bb * nq + i, 0)),
                  pl.BlockSpec((nb, KV_W), lambda bb, i: (bb, 0)),
                  pl.BlockSpec((t, KV_W), lambda bb, i: (bb, 0)),
                  pl.BlockSpec((t, KV_W), lambda bb, i: (bb, 0)),
                  pl.BlockSpec((tq, LANES), lambda bb, i: (bb * nq + i, 0)),
                  pl.BlockSpec((nsb, t), lambda bb, i: (0, 0))],
        out_specs=pl.BlockSpec((tq, 512), lambda bb, i: (bb * nq + i, 0)),
        scratch_shapes=[pltpu.VMEM((tq, t), F32), pltpu.VMEM((m_rows, 1), F32), pltpu.VMEM((m_rows, 1), F32),
                        pltpu.VMEM((m_rows, HEAD_DIM), F32)],
        compiler_params=_cparams(("parallel", "arbitrary")),
        name="nsa_prompt",
    )(q, kcv, slc, win, gates, expand)


def _gla_level_matrix(tt):
    lv = int(math.log2(tt))
    m = np.zeros((lv, tt, tt), np.float32)
    t = np.arange(tt)
    for l in range(lv):
        half = 1 << l
        split = ((t >> (l + 1)) << (l + 1)) + half - 1
        u = np.arange(tt)[None, :]
        upper = t > split
        m[l] = np.where(upper[:, None], (u > split[:, None]) & (u <= t[:, None]),
                        (u > t[:, None]) & (u <= split[:, None]))
    return m.reshape(lv * tt, tt)


def _ret_log_decay():
    return [float(np.log(np.float32(1.0) - np.float32(2.0) ** np.float32(-5.0 - h))) for h in range(RET_HEADS)]


def _recur_kernel(gla_ref, ret_ref, gn_ref, lvl_ref, tril_ref, sg0_ref, sr0_ref,
                  o_ref, sg_out_ref, sr_out_ref, sg_ref, sr_ref, *, tt, n_valid):
    ti = pl.program_id(1)
    nt = pl.num_programs(1)
    levels = int(math.log2(tt))

    @pl.when(ti == 0)
    def _():
        sg_ref[...] = sg0_ref[...]
        sr_ref[...] = sr0_ref[...]

    rowi = lax.broadcasted_iota(jnp.int32, (tt, 1), 0)
    coli = lax.broadcasted_iota(jnp.int32, (1, tt), 1)

    gq = gla_ref[:, 0:128]
    gk = gla_ref[:, 128:256]
    la = gla_ref[:, 256:384]
    gv = gla_ref[:, 384:640]
    gr = gla_ref[:, 640:896]
    if n_valid < tt:
        la = jnp.where(rowi < n_valid, la, 0.0)
    parts = _split3(la)
    tril = tril_ref[...]
    lvl = lvl_ref[...]
    ones = jnp.ones((tt, GLA_DV), BF16)
    cum = sum(_dot(tril, pt) for pt in parts)
    dlv = sum(_dot(lvl, pt) for pt in parts)
    tot = sum(_dot_tn(pt, ones) for pt in parts)
    q_dec = gq * jnp.exp(cum)
    k_dec = gk * jnp.exp(cum[tt - 1:tt, :] - cum)
    gla_out = []
    for h in range(GLA_HEADS):
        sl = slice(h * GLA_DK, (h + 1) * GLA_DK)
        vs = slice(h * GLA_DV, (h + 1) * GLA_DV)
        qh, kh = gq[:, sl], gk[:, sl]
        vh = gv[:, vs].astype(BF16)
        attn = jnp.where(rowi == coli, _dot_nt(qh.astype(BF16), kh.astype(BF16)), 0.0)
        for l in range(levels):
            ed = jnp.exp(dlv[l * tt:(l + 1) * tt, sl])
            upper = ((rowi >> l) & 1) == 1
            qe = jnp.where(upper, qh * ed, 0.0).astype(BF16)
            ke = jnp.where(upper, 0.0, kh * ed).astype(BF16)
            same = (rowi >> (l + 1)) == (coli >> (l + 1))
            attn = attn + jnp.where(same, _dot_nt(qe, ke), 0.0)
        s_h = sg_ref[h]
        o = _dot(attn.astype(BF16), vh) + _dot(q_dec[:, sl].astype(BF16), s_h.astype(BF16))
        sg_ref[h] = jnp.exp(tot[sl, :]) * s_h + _dot_tn(k_dec[:, sl].astype(BF16), vh)
        gla_out.append(_rms(o) * gn_ref[...] * gr[:, vs])
    o_ref[:, 0:256] = jnp.concatenate(gla_out, axis=1)

    rel = (rowi - coli).astype(F32)
    pos1 = (rowi + 1).astype(F32)
    left = (n_valid - 1 - rowi).astype(F32)
    ret_out = []
    for h, lg in enumerate(_ret_log_decay()):
        sl = slice(h * RET_DIM, (h + 1) * RET_DIM)
        qh = ret_ref[:, sl]
        kh = ret_ref[:, 256 + h * RET_DIM:256 + (h + 1) * RET_DIM]
        vh = ret_ref[:, 512 + h * RET_DIM:512 + (h + 1) * RET_DIM].astype(BF16)
        gh = ret_ref[:, 768 + h * RET_DIM:768 + (h + 1) * RET_DIM]
        dmat = jnp.where(rel >= 0.0, jnp.exp(lg * jnp.maximum(rel, 0.0)), 0.0)
        attn = _dot_nt(qh.astype(BF16), kh.astype(BF16)) * dmat
        s_h = sr_ref[h]
        o = _dot(attn.astype(BF16), vh) + _dot(qh.astype(BF16), s_h.astype(BF16)) * jnp.exp(lg * pos1)
        sr_ref[h] = math.exp(lg * n_valid) * s_h + _dot_tn((kh * jnp.exp(lg * left)).astype(BF16), vh)
        ret_out.append(_rms(o) * gh)
    o_ref[:, 256:512] = jnp.concatenate(ret_out, axis=1)

    @pl.when(ti == nt - 1)
    def _():
        sg_out_ref[...] = sg_ref[...]
        sr_out_ref[...] = sr_ref[...]


def _recurrent(gla, ret, gnorm, sg0, sr0, *, b, t, tt, n_valid=None):
    n = b * t
    nt = t // tt
    n_valid = tt if n_valid is None else n_valid
    assert n_valid == tt or nt == 1
    lvl = jnp.asarray(_gla_level_matrix(tt), BF16)
    tril = jnp.asarray(np.tril(np.ones((tt, tt), np.float32)), BF16)
    full2 = lambda a: pl.BlockSpec(a.shape, lambda bb, i: (0, 0))
    st = lambda a: pl.BlockSpec((None,) + a.shape[1:], lambda bb, i: (bb, 0, 0, 0))
    return pl.pallas_call(
        functools.partial(_recur_kernel, tt=tt, n_valid=n_valid),
        out_shape=(jax.ShapeDtypeStruct((n, 512), F32), jax.ShapeDtypeStruct(sg0.shape, F32),
                   jax.ShapeDtypeStruct(sr0.shape, F32)),
        grid=(b, nt),
        in_specs=[pl.BlockSpec((tt, GLA_W), lambda bb, i: (bb * nt + i, 0)),
                  pl.BlockSpec((tt, RET_W), lambda bb, i: (bb * nt + i, 0)),
                  full2(gnorm), full2(lvl), full2(tril), st(sg0), st(sr0)],
        out_specs=(pl.BlockSpec((tt, 512), lambda bb, i: (bb * nt + i, 0)), st(sg0), st(sr0)),
        scratch_shapes=[pltpu.VMEM(sg0.shape[1:], F32), pltpu.VMEM(sr0.shape[1:], F32)],
        compiler_params=_cparams(("parallel", "arbitrary")),
        name="recurrent",
    )(gla, ret, gnorm, lvl, tril, sg0, sr0)


def _outproj_kernel(x_ref, a_ref, b_ref, w_ref, o_ref):
    ka = a_ref.shape[1]
    o_ref[...] = (x_ref[...] + _dot(a_ref[...].astype(BF16), w_ref[0:ka, :])
                  + _dot(b_ref[...].astype(BF16), w_ref[ka:, :]))


def _outproj(x, a, b_, w, *, tm):
    n = x.shape[0]
    row = lambda a_: pl.BlockSpec((tm, a_.shape[1]), lambda i: (i, 0))
    return pl.pallas_call(
        _outproj_kernel,
        out_shape=jax.ShapeDtypeStruct(x.shape, F32),
        grid=(n // tm,),
        in_specs=[row(x), row(a), row(b_), pl.BlockSpec(w.shape, lambda i: (0, 0))],
        out_specs=row(x),
        compiler_params=_cparams(("parallel",)),
        name="outproj",
    )(x, a, b_, w)


def _cross_kernel(x_ref, gain_ref, wq_ref, wo_ref, kv_ref, o_ref):
    x = x_ref[...]
    xb = (_rms(x) * gain_ref[...]).astype(BF16)
    q = _dot(xb, wq_ref[...])
    heads = []
    for h in range(MEM_HEADS):
        qh = q[:, h * MEM_HD:(h + 1) * MEM_HD].astype(BF16)
        kh = kv_ref[:, h * MEM_HD:(h + 1) * MEM_HD].astype(BF16)
        vh = kv_ref[:, (MEM_HEADS + h) * MEM_HD:(MEM_HEADS + h + 1) * MEM_HD].astype(BF16)
        s = _dot_nt(qh, kh) * (MEM_HD ** -0.5)
        e = jnp.exp(s - jnp.max(s, axis=-1, keepdims=True))
        p = e / jnp.sum(e, axis=-1, keepdims=True)
        heads.append(_dot(p.astype(BF16), vh))
    att = jnp.concatenate(heads, axis=1).astype(BF16)
    o_ref[...] = x + _dot(att, wo_ref[...])


def _cross(x, gain, wq, wo, memkv, *, b, t, tq):
    nq = t // tq
    n_mem = memkv.shape[0] // b
    full = lambda a: pl.BlockSpec(a.shape, lambda bb, i: (0, 0))
    row = pl.BlockSpec((tq, D_MODEL), lambda bb, i: (bb * nq + i, 0))
    return pl.pallas_call(
        _cross_kernel,
        out_shape=jax.ShapeDtypeStruct(x.shape, F32),
        grid=(b, nq),
        in_specs=[row, full(gain), full(wq), full(wo),
                  pl.BlockSpec((n_mem, memkv.shape[1]), lambda bb, i: (bb, 0))],
        out_specs=row,
        compiler_params=_cparams(("parallel", "arbitrary")),
        name="cross",
    )(x, gain, wq, wo, memkv)


def _ffn_kernel(x_ref, gain_ref, w1_ref, w2_ref, gf_ref, o_ref, xn_ref, acc_ref, *, final_norm):
    j = pl.program_id(1)

    @pl.when(j == 0)
    def _():
        xn_ref[...] = (_rms(x_ref[...]) * gain_ref[...]).astype(BF16)
        acc_ref[...] = x_ref[...]

    h = jnp.maximum(_dot(xn_ref[...], w1_ref[...]), 0.0)
    acc_ref[...] += _dot((h * h).astype(BF16), w2_ref[...])

    @pl.when(j == pl.num_programs(1) - 1)
    def _():
        y = acc_ref[...]
        if final_norm:
            y = _rms(y) * gf_ref[...]
        o_ref[...] = y


def _ffn(x, gain, w1, w2, gfinal, *, tm, tf, final_norm):
    n = x.shape[0]
    row = pl.BlockSpec((tm, D_MODEL), lambda i, j: (i, 0))
    vec = pl.BlockSpec((1, D_MODEL), lambda i, j: (0, 0))
    return pl.pallas_call(
        functools.partial(_ffn_kernel, final_norm=final_norm),
        out_shape=jax.ShapeDtypeStruct(x.shape, F32),
        grid=(n // tm, D_FF // tf),
        in_specs=[row, vec, pl.BlockSpec((D_MODEL, tf), lambda i, j: (0, j)),
                  pl.BlockSpec((tf, D_MODEL), lambda i, j: (j, 0)), vec],
        out_specs=row,
        scratch_shapes=[pltpu.VMEM((tm, D_MODEL), BF16), pltpu.VMEM((tm, D_MODEL), F32)],
        compiler_params=_cparams(("parallel", "arbitrary")),
        name="ffn",
    )(x, gain, w1, w2, gfinal)


def _matmul_kernel(x_ref, w_ref, o_ref):
    o_ref[...] = _dot(x_ref[...].astype(BF16), w_ref[...])


def _matmul(x, w, *, tm):
    n, k = x.shape
    return pl.pallas_call(
        _matmul_kernel,
        out_shape=jax.ShapeDtypeStruct((n, w.shape[1]), F32),
        grid=(n // tm,),
        in_specs=[pl.BlockSpec((tm, k), lambda i: (i, 0)), pl.BlockSpec(w.shape, lambda i: (0, 0))],
        out_specs=pl.BlockSpec((tm, w.shape[1]), lambda i: (i, 0)),
        compiler_params=_cparams(("parallel",)),
        name="memkv",
    )(x, w)


TS = 8


def _cmp_pages_kernel(pt_ref, pool_ref, pe_ref, w1_ref, w2_ref, o_ref, buf_ref, sem_ref, *, pg, base):
    s = pl.program_id(0)
    ns = pl.num_programs(0)
    rows_per_page = pool_ref.shape[1]

    def page_copy(page, slot, j):
        return pltpu.make_async_copy(pool_ref.at[page],
                                     buf_ref.at[slot, pl.ds(j * rows_per_page, rows_per_page), :],
                                     sem_ref.at[slot])

    def fetch(step, slot):
        def body(j, c):
            page_copy(base + pt_ref[step * pg + j], slot, j).start()
            return c
        lax.fori_loop(0, pg, body, 0)

    @pl.when(s == 0)
    def _():
        fetch(0, 0)

    @pl.when(s + 1 < ns)
    def _():
        fetch(s + 1, lax.rem(s + 1, 2))

    slot = lax.rem(s, 2)

    def wait_body(j, c):
        page_copy(0, slot, j).wait()
        return c
    lax.fori_loop(0, pg, wait_body, 0)

    half = pe_ref.shape[1] // 2
    hid = w1_ref.shape[1]
    nrow = pg * rows_per_page
    nblk = nrow // 2
    rowi = lax.broadcasted_iota(jnp.int32, (nrow, 1), 0)
    first = (rowi & 1) == 0
    x = buf_ref[slot] + jnp.where(first, pe_ref[:, 0:half], pe_ref[:, half:])
    xb = x.astype(BF16)
    y = jnp.where(first, _dot(xb, w1_ref[0:half, :]), _dot(xb, w1_ref[half:, :]))
    out_r = lax.broadcasted_iota(jnp.int32, (nblk, 1), 0)
    blk = jnp.where(out_r < nblk // 2, 2 * out_r, 2 * (out_r - nblk // 2) + 1)
    src = lax.broadcasted_iota(jnp.int32, (1, nrow), 1)
    pick = jnp.where((src >> 1) == blk, 1.0, 0.0).astype(BF16)
    acc = jnp.zeros((nblk, hid), F32)
    for part in _split3(y):
        acc = acc + _dot(pick, part)
    o_ref[...] = _dot(_gelu_tanh(acc).astype(BF16), w2_ref[...])


def _compress_pages(page_table_flat, pool, pe_big, w1_big, w2_big, *, layer, pg):
    depth, n_pool = pool.shape[:2]
    blocks_per_page = PAGE_SIZE // CMP_BLOCK
    half = CMP_BLOCK * KV_W // 2
    pool_v = pool.reshape(depth * n_pool, 2 * blocks_per_page, half)
    n_pages = page_table_flat.shape[0]
    full = lambda a: pl.BlockSpec(a.shape, lambda i, pt: (0, 0))
    return pl.pallas_call(
        functools.partial(_cmp_pages_kernel, pg=pg, base=layer * n_pool),
        out_shape=jax.ShapeDtypeStruct((n_pages * blocks_per_page, KV_W), F32),
        grid_spec=pltpu.PrefetchScalarGridSpec(
            num_scalar_prefetch=1,
            grid=(n_pages // pg,),
            in_specs=[pl.BlockSpec(memory_space=pl.ANY), full(pe_big), full(w1_big), full(w2_big)],
            out_specs=pl.BlockSpec((pg * blocks_per_page, KV_W), lambda i, pt: (i, 0)),
            scratch_shapes=[pltpu.VMEM((2, pg * 2 * blocks_per_page, half), F32), pltpu.SemaphoreType.DMA((2,))]),
        compiler_params=_cparams(("arbitrary",)),
        name="compress_pages",
    )(page_table_flat, pool_v, pe_big, w1_big, w2_big)


def _heads_to_lanes(o, ts):
    return jnp.concatenate([o[r * ts:(r + 1) * ts] for r in range(NSA_REP)], axis=1)


def _nsa_sample_a_kernel(q_ref, kcp_ref, kcn_ref, wc_ref, wn_ref, ocmp_ref, owin_ref, idx_ref, *, past, n_new_blk,
                         chunk):
    ts = q_ref.shape[0]
    q = q_ref[...].astype(F32)
    nbp = kcp_ref.shape[0]
    hb = nbp // 2
    hc = chunk // 2
    nsb = hb + (n_new_blk + 1) // 2
    wb = wc_ref.shape[0]
    m_rows = NSA_REP * ts
    row = lax.broadcasted_iota(jnp.int32, (m_rows, 1), 0)
    tpos = past + lax.rem(row, ts)
    t8 = past + lax.broadcasted_iota(jnp.int32, (ts, 1), 0)
    k_past = kcp_ref[...]
    k_new = kcn_ref[...]
    w_old = wc_ref[...]
    w_new = wn_ref[...]
    nn = k_new.shape[0]
    cp = lax.broadcasted_iota(jnp.int32, (1, nbp), 1)
    cn = lax.broadcasted_iota(jnp.int32, (1, nn), 1)
    within = lax.rem(cp, chunk)
    blk_p = (cp // chunk) * chunk + 2 * lax.rem(within, hc) + within // hc
    mask_p = (blk_p + 1) * CMP_BLOCK - 1 <= tpos
    mask_n = ((nbp + cn + 1) * CMP_BLOCK - 1 <= tpos) & (cn < n_new_blk)
    kp_old = past - wb + lax.broadcasted_iota(jnp.int32, (1, wb), 1)
    kp_new = past + lax.broadcasted_iota(jnp.int32, (1, w_new.shape[0]), 1)
    wmask_old = (kp_old <= tpos) & (kp_old > tpos - WINDOW) & (kp_old >= 0)
    wmask_new = (kp_new <= tpos) & (kp_new > tpos - WINDOW)
    width = hb + LANES
    j = lax.broadcasted_iota(jnp.int32, (1, width), 1)
    lane = lax.broadcasted_iota(jnp.int32, (1, LANES), 1)

    def joint_softmax(parts):
        mx = None
        for s, mk in parts:
            cur = jnp.max(jnp.where(mk, s, NEG), axis=-1, keepdims=True)
            mx = cur if mx is None else jnp.maximum(mx, cur)
        mx = jnp.where(mx > 0.5 * NEG, mx, 0.0)
        es = [jnp.where(mk, jnp.exp(jnp.where(mk, s, NEG) - mx), 0.0) for s, mk in parts]
        den = sum(jnp.sum(e, axis=-1, keepdims=True) for e in es)
        inv = 1.0 / jnp.where(den > 0.0, den, 1.0)
        return [e * inv for e in es]

    def fold_heads(p):
        out = p[0:ts]
        for r in range(1, NSA_REP):
            out = out + p[r * ts:(r + 1) * ts]
        return out

    for g in range(NSA_KV_HEADS):
        qg = _stack_heads(q, g).astype(BF16)
        kcol = slice(g * HEAD_DIM, (g + 1) * HEAD_DIM)
        vcol = slice((NSA_KV_HEADS + g) * HEAD_DIM, (NSA_KV_HEADS + g + 1) * HEAD_DIM)

        p_p, p_n = joint_softmax([(_dot_nt(qg, k_past[:, kcol].astype(BF16)), mask_p),
                                  (_dot_nt(qg, k_new[:, kcol].astype(BF16)), mask_n)])
        o_cmp = (_dot(p_p.astype(BF16), k_past[:, vcol].astype(BF16))
                 + _dot(p_n.astype(BF16), k_new[:, vcol].astype(BF16)))
        ocmp_ref[:, g * NSA_REP * HEAD_DIM:(g + 1) * NSA_REP * HEAD_DIM] = _heads_to_lanes(o_cmp, ts)

        pp = fold_heads(p_p)
        imp_past = jnp.concatenate([pp[:, c * chunk:c * chunk + hc] + pp[:, c * chunk + hc:(c + 1) * chunk]
                                    for c in range(nbp // chunk)], axis=1)
        pn = fold_heads(p_n)
        imp_new = jnp.zeros((ts, LANES), F32)
        for c in range(n_new_blk):
            imp_new = imp_new + jnp.where(lane == c // 2, pn[:, c:c + 1], 0.0)
        imp = jnp.concatenate([imp_past, imp_new], axis=1)
        cur = t8 // SEL_BLOCK
        forced = (j == 0) | (j == cur) | (j == cur - 1)
        valid = j * SEL_BLOCK <= t8
        score = jnp.where(forced, -NEG, jnp.where(valid, imp, NEG))
        alive = j < nsb
        chosen = jnp.zeros((ts, LANES), jnp.int32)
        for n in range(min(N_SELECT, nsb)):
            best = jnp.max(jnp.where(alive, score, 2.0 * NEG), axis=-1, keepdims=True)
            pick = jnp.min(jnp.where(alive & (score == best), j, width), axis=-1, keepdims=True)
            alive = alive & (j != pick)
            chosen = jnp.where(lane == n, pick, chosen)
        idx_ref[g] = chosen

        pw_old, pw_new = joint_softmax([(_dot_nt(qg, w_old[:, kcol].astype(BF16)), wmask_old),
                                        (_dot_nt(qg, w_new[:, kcol].astype(BF16)), wmask_new)])
        o_win = (_dot(pw_old.astype(BF16), w_old[:, vcol].astype(BF16))
                 + _dot(pw_new.astype(BF16), w_new[:, vcol].astype(BF16)))
        owin_ref[:, g * NSA_REP * HEAD_DIM:(g + 1) * NSA_REP * HEAD_DIM] = _heads_to_lanes(o_win, ts)


def _nsa_sample_a(q, kc_past, kc_new, win_old, win_new, *, b, past, n_new_blk, chunk):
    nbp = kc_past.shape[0] // b
    blk3 = lambda a: pl.BlockSpec((None,) + a.shape[1:], lambda bb: (bb, 0, 0))
    return pl.pallas_call(
        functools.partial(_nsa_sample_a_kernel, past=past, n_new_blk=n_new_blk, chunk=chunk),
        out_shape=(jax.ShapeDtypeStruct((b, TS, 512), F32), jax.ShapeDtypeStruct((b, TS, 512), F32),
                   jax.ShapeDtypeStruct((b, NSA_KV_HEADS, TS, LANES), jnp.int32)),
        grid=(b,),
        in_specs=[blk3(q), pl.BlockSpec((nbp, KV_W), lambda bb: (bb, 0)), blk3(kc_new), blk3(win_old),
                  blk3(win_new)],
        out_specs=(pl.BlockSpec((None, TS, 512), lambda bb: (bb, 0, 0)),
                   pl.BlockSpec((None, TS, 512), lambda bb: (bb, 0, 0)),
                   pl.BlockSpec((None, NSA_KV_HEADS, TS, LANES), lambda bb: (bb, 0, 0, 0))),
        compiler_params=_cparams(("parallel",)),
        name="nsa_sample_a",
    )(q, kc_past, kc_new, win_old, win_new)


def _nsa_sel_kernel(idx_ref, pt_ref, pool_ref, new_ref, q_ref, ocmp_ref, owin_ref, gate_ref, o_ref,
                    buf_ref, sem_ref, *, past, base, n_pages, tv, nsel):
    bb = pl.program_id(0)
    nb = pl.num_programs(0)
    ts = q_ref.shape[0]
    blocks_per_page = PAGE_SIZE // SEL_BLOCK
    nsb_past = past // SEL_BLOCK
    new_rows = new_ref.shape[0] // nb

    def slot_rows(slot, g, t, n):
        return buf_ref.at[slot, pl.ds(((g * tv + t) * nsel + n) * SEL_BLOCK, SEL_BLOCK), :]

    def block_id(b_, g, t, n):
        return idx_ref[((b_ * NSA_KV_HEADS + g) * ts + t) * nsel + n]

    def fetch(b_, slot):
        for g in range(NSA_KV_HEADS):
            for t in range(tv):
                def body(n, c):
                    bid = block_id(b_, g, t, n)
                    dst = slot_rows(slot, g, t, n)

                    @pl.when(bid < nsb_past)
                    def _():
                        page = base + pt_ref[b_ * n_pages + bid // blocks_per_page]
                        r0 = pl.multiple_of(page * PAGE_SIZE + lax.rem(bid, blocks_per_page) * SEL_BLOCK, SEL_BLOCK)
                        pltpu.make_async_copy(pool_ref.at[pl.ds(r0, SEL_BLOCK), :], dst, sem_ref.at[slot]).start()

                    @pl.when(bid >= nsb_past)
                    def _():
                        r0 = pl.multiple_of(b_ * new_rows + (bid - nsb_past) * SEL_BLOCK, SEL_BLOCK)
                        pltpu.make_async_copy(new_ref.at[pl.ds(r0, SEL_BLOCK), :], dst, sem_ref.at[slot]).start()
                    return c
                lax.fori_loop(0, nsel, body, 0)

    @pl.when(bb == 0)
    def _():
        fetch(0, 0)

    @pl.when(bb + 1 < nb)
    def _():
        fetch(bb + 1, lax.rem(bb + 1, 2))

    slot = lax.rem(bb, 2)

    def wait_body(n, c):
        pltpu.make_async_copy(pool_ref.at[pl.ds(0, SEL_BLOCK), :], slot_rows(slot, 0, 0, 0), sem_ref.at[slot]).wait()
        return c
    lax.fori_loop(0, NSA_KV_HEADS * tv * nsel, wait_body, 0)

    q = q_ref[...].astype(F32)
    gates = gate_ref[...]
    o_cmp = ocmp_ref[...]
    o_win = owin_ref[...]
    m_rows = NSA_REP * ts
    row = lax.broadcasted_iota(jnp.int32, (m_rows, 1), 0)
    trow = lax.rem(row, ts)
    tpos = past + trow
    nk = nsel * SEL_BLOCK
    lane = lax.broadcasted_iota(jnp.int32, (1, nk), 1)
    outs = []
    for g in range(NSA_KV_HEADS):
        qg = _stack_heads(q, g).astype(BF16)
        kcol = slice(g * HEAD_DIM, (g + 1) * HEAD_DIM)
        vcol = slice((NSA_KV_HEADS + g) * HEAD_DIM, (NSA_KV_HEADS + g + 1) * HEAD_DIM)
        o_sel = jnp.zeros((m_rows, HEAD_DIM), F32)
        for t in range(tv):
            r0 = (g * tv + t) * nk
            kb = buf_ref[slot, r0:r0 + nk, kcol].astype(BF16)
            vb = buf_ref[slot, r0:r0 + nk, vcol].astype(BF16)
            kpos = lax.rem(lane, SEL_BLOCK)
            for n in range(nsel):
                kpos = kpos + jnp.where(lane // SEL_BLOCK == n, block_id(bb, g, t, n) * SEL_BLOCK, 0)
            mask = kpos <= tpos
            s = jnp.where(mask, _dot_nt(qg, kb), NEG)
            e = jnp.where(mask, jnp.exp(s - jnp.max(s, axis=-1, keepdims=True)), 0.0)
            p = e / jnp.sum(e, axis=-1, keepdims=True)
            o_sel = o_sel + jnp.where(trow == t, _dot(p.astype(BF16), vb), 0.0)
        for r in range(NSA_REP):
            h = g * NSA_REP + r
            hs = slice(h * HEAD_DIM, (h + 1) * HEAD_DIM)
            outs.append(o_cmp[:, hs] * gates[:, 3 * h:3 * h + 1] + o_sel[r * ts:(r + 1) * ts] * gates[:, 3 * h + 1:3 * h + 2]
                        + o_win[:, hs] * gates[:, 3 * h + 2:3 * h + 3])
    o_ref[...] = jnp.concatenate(outs, axis=1)


def _nsa_sample_sel(idx_flat, page_table_flat, pool, slc_new, q, o_cmp, o_win, gates, *, b, past, layer, tv):
    depth, n_pool = pool.shape[:2]
    pool_v = pool.reshape(depth * n_pool * PAGE_SIZE, KV_W)
    nsel = idx_flat.shape[0] // (b * NSA_KV_HEADS * TS)
    blk3 = lambda a: pl.BlockSpec((None,) + a.shape[1:], lambda bb, i_, p_: (bb, 0, 0))
    return pl.pallas_call(
        functools.partial(_nsa_sel_kernel, past=past, base=layer * n_pool, n_pages=page_table_flat.shape[0] // b,
                          tv=tv, nsel=nsel),
        out_shape=jax.ShapeDtypeStruct((b, TS, 512), F32),
        grid_spec=pltpu.PrefetchScalarGridSpec(
            num_scalar_prefetch=2,
            grid=(b,),
            in_specs=[pl.BlockSpec(memory_space=pl.ANY), pl.BlockSpec(memory_space=pl.ANY),
                      blk3(q), blk3(o_cmp), blk3(o_win), blk3(gates)],
            out_specs=pl.BlockSpec((None, TS, 512), lambda bb, i_, p_: (bb, 0, 0)),
            scratch_shapes=[pltpu.VMEM((2, NSA_KV_HEADS * tv * nsel * SEL_BLOCK, KV_W), F32),
                            pltpu.SemaphoreType.DMA((2,))]),
        compiler_params=_cparams(("arbitrary",)),
        name="nsa_sample_sel",
    )(idx_flat, page_table_flat, pool_v, slc_new, q, o_cmp, o_win, gates)


def _layer_weights(l, norm_mix, w_in, gla_wa2, gla_ba, gla_norm, cmp_pe, cmp_w1, cmp_w2, w_out, norm_mem,
                   w_mq, w_mkv, w_mo, norm_ffn, w_ff1, w_ff2):
    pe_big, w1_big, w2_big = _pack_compress(cmp_pe[l], cmp_w1[l], cmp_w2[l])
    return dict(
        norm_mix=norm_mix[l][None, :], w_in=_pack_w_in(w_in[l]), wa2=_pack_wa2(gla_wa2[l]),
        ba=gla_ba[l][None, :], gnorm=gla_norm[l][None, :], pe_big=pe_big, w1_big=w1_big, w2_big=w2_big,
        w_out=w_out[l].astype(BF16), norm_mem=norm_mem[l][None, :], w_mq=w_mq[l].astype(BF16),
        w_mkv=w_mkv[l].astype(BF16), w_mo=w_mo[l].astype(BF16), norm_ffn=norm_ffn[l][None, :],
        w_ff1=w_ff1[l].astype(BF16), w_ff2=w_ff2[l].astype(BF16))


def _even_odd(kcv, b):
    nb = kcv.shape[0] // b
    return kcv.reshape(b, nb // 2, 2, -1).transpose(0, 2, 1, 3).reshape(b * nb, -1)


def _prompt_layer(x, mem, p, tabs, gfinal, *, b, t, final_norm):
    q, cmp_rows, slc_rows, win_rows, gates, gla, ret = _project(
        x, p['norm_mix'], p['w_in'], p['wa2'], p['ba'], tabs, tm=256)
    kcv = _compress(cmp_rows.reshape(-1, CMP_BLOCK * KV_W), p['pe_big'], p['w1_big'], p['w2_big'], tr=128)
    o_nsa = _nsa_prompt(q, _even_odd(kcv, b), slc_rows, win_rows, gates, b=b, t=t, tq=128, tk=128)
    sg0 = jnp.zeros((b, GLA_HEADS, GLA_DK, GLA_DV), F32)
    sr0 = jnp.zeros((b, RET_HEADS, RET_DIM, RET_DIM), F32)
    o_rec, s_gla, s_ret = _recurrent(gla, ret, p['gnorm'], sg0, sr0, b=b, t=t, tt=128)
    x = _outproj(x, o_nsa, o_rec, p['w_out'], tm=512)
    mem_kv = _matmul(mem, p['w_mkv'], tm=256)
    x = _cross(x, p['norm_mem'], p['w_mq'], p['w_mo'], mem_kv, b=b, t=t, tq=256)
    x = _ffn(x, p['norm_ffn'], p['w_ff1'], p['w_ff2'], gfinal, tm=512, tf=1024, final_norm=final_norm)
    wlen = min(WINDOW, t)
    win_tail = win_rows.reshape(b, t, KV_W)[:, t - wlen:]
    return x, (cmp_rows, slc_rows, win_tail, s_gla, s_ret, mem_kv)


def _sample_layer(x, l, p, tabs, gfinal, cache_cmp_kv, cache_slc_kv, cache_win_kv, state_gla, state_ret,
                  cache_mem_kv, pt_flat, *, b, past, tv, final_norm):
    n = b * TS
    q, cmp_rows, slc_rows, win_rows, gates, gla, ret = _project(
        x, p['norm_mix'], p['w_in'], p['wa2'], p['ba'], tabs, tm=n, valid_period=(TS, tv))
    rows3 = lambda a: a.reshape(b, TS, a.shape[-1])
    pad_blk = lambda a: jnp.pad(rows3(a), ((0, 0), (0, SEL_BLOCK - TS), (0, 0)))
    n_new_blk = SEL_BLOCK // CMP_BLOCK
    pg = min(64, past // PAGE_SIZE)
    kc_past = _compress_pages(pt_flat, cache_cmp_kv, p['pe_big'], p['w1_big'], p['w2_big'], layer=l, pg=pg)
    kc_new = _compress(pad_blk(cmp_rows).reshape(b * n_new_blk, CMP_BLOCK * KV_W), p['pe_big'], p['w1_big'],
                       p['w2_big'], tr=b * n_new_blk)
    kc_new = jnp.pad(kc_new.reshape(b, n_new_blk, KV_W), ((0, 0), (0, TS - n_new_blk), (0, 0)))
    win_old = cache_win_kv[l].reshape(b, -1, KV_W)
    q3 = rows3(q)
    o_cmp, o_win, idx = _nsa_sample_a(q3, kc_past, kc_new, win_old, rows3(win_rows), b=b, past=past,
                                      n_new_blk=n_new_blk, chunk=pg * (PAGE_SIZE // CMP_BLOCK))
    nsel = min(N_SELECT, past // SEL_BLOCK + 1)
    o_nsa = _nsa_sample_sel(idx[..., :nsel].reshape(-1), pt_flat, cache_slc_kv,
                            pad_blk(slc_rows).reshape(b * SEL_BLOCK, KV_W), q3, o_cmp, o_win, rows3(gates),
                            b=b, past=past, layer=l, tv=tv)
    o_rec, s_gla, s_ret = _recurrent(gla, ret, p['gnorm'], state_gla[l], state_ret[l], b=b, t=TS, tt=TS, n_valid=tv)
    x = _outproj(x, o_nsa.reshape(n, -1), o_rec, p['w_out'], tm=n)
    x = _cross(x, p['norm_mem'], p['w_mq'], p['w_mo'], cache_mem_kv[l].reshape(-1, 2 * D_MODEL), b=b, t=TS, tq=TS)
    x = _ffn(x, p['norm_ffn'], p['w_ff1'], p['w_ff2'], gfinal, tm=n, tf=1024, final_norm=final_norm)
    new_win = jnp.concatenate([win_old[:, tv:], rows3(win_rows)[:, :tv]], axis=1)
    return x, (rows3(cmp_rows)[:, :tv], rows3(slc_rows)[:, :tv], new_win, s_gla, s_ret)


def kernel(x_prompt, x_sample, mem_prompt, cache_cmp_kv, cache_slc_kv, cache_win_kv, state_gla, state_ret,
           cache_mem_kv, page_table, norm_mix, w_in, gla_wa2, gla_ba, gla_norm, cmp_pe, cmp_w1, cmp_w2, w_out,
           norm_mem, w_mq, w_mkv, w_mo, norm_ffn, w_ff1, w_ff2, norm_final):
    depth = w_in.shape[0]
    b, t, d = x_prompt.shape
    g, hd = NSA_KV_HEADS, HEAD_DIM
    gfinal = norm_final[None, :]
    tabs_p = _all_rope_tables(jnp.arange(t))
    hp = x_prompt.reshape(b * t, d)
    mem = mem_prompt.reshape(-1, d)
    sb, tv = x_sample.shape[:2]
    assert tv <= TS
    past = page_table.shape[1] * PAGE_SIZE
    tabs_s = _all_rope_tables(past + jnp.arange(sb * TS) % TS)
    hs = jnp.pad(x_sample, ((0, 0), (0, TS - tv), (0, 0))).reshape(sb * TS, d)
    pt_flat = page_table.reshape(-1)
    new_p = [[] for _ in range(6)]
    new_s = [[] for _ in range(5)]
    for l in range(depth):
        p = _layer_weights(l, norm_mix, w_in, gla_wa2, gla_ba, gla_norm, cmp_pe, cmp_w1, cmp_w2, w_out, norm_mem,
                           w_mq, w_mkv, w_mo, norm_ffn, w_ff1, w_ff2)
        last = l == depth - 1
        hp, st = _prompt_layer(hp, mem, p, tabs_p, gfinal, b=b, t=t, final_norm=last)
        for lst, a in zip(new_p, st):
            lst.append(a)
        hs, st = _sample_layer(hs, l, p, tabs_s, gfinal, cache_cmp_kv, cache_slc_kv, cache_win_kv, state_gla,
                               state_ret, cache_mem_kv, pt_flat, b=sb, past=past, tv=tv, final_norm=last)
        for lst, a in zip(new_s, st):
            lst.append(a)
    cmp_p, slc_p, win_p, gla_p, ret_p, mem_p = [jnp.stack(a) for a in new_p]
    cmp_s, slc_s, win_s, gla_s, ret_s = [jnp.stack(a) for a in new_s]
    wlen = win_p.shape[2]
    n_mem = mem_prompt.shape[1]
    kv6 = lambda a: a.reshape(a.shape[:3] + (2, g, hd))
    return (hp.reshape(b, t, d), hs.reshape(sb, TS, d)[:, :tv],
            kv6(cmp_p.reshape(depth, b, t, KV_W)), kv6(slc_p.reshape(depth, b, t, KV_W)), kv6(win_p), gla_p, ret_p,
            mem_p.reshape(depth, b, n_mem, 2, MEM_HEADS, MEM_HD),
            kv6(cmp_s), kv6(slc_s), kv6(win_s), gla_s, ret_s)
```

```python
import functools
import math

import numpy as np
import jax
import jax.numpy as jnp
from jax import lax
from jax.experimental import pallas as pl
from jax.experimental.pallas import tpu as pltpu

F32 = jnp.float32
BF16 = jnp.bfloat16

D_MODEL = 1024
PAGE_SIZE = 128
HEAD_DIM = 64
NSA_HEADS = 8
NSA_KV_HEADS = 2
NSA_REP = NSA_HEADS // NSA_KV_HEADS
CMP_BLOCK = 32
SEL_BLOCK = 64
N_SELECT = 16
WINDOW = 512
ROT_DIM = HEAD_DIM // 4
ROPE_THETA = 500000.0
GLA_HEADS = 4
GLA_DK = 32
GLA_DV = 64
GLA_RANK = 16
GLA_TAU = 16.0
RET_HEADS = 4
RET_DIM = 64
RET_THETA = 10000.0
MEM_HEADS = 4
MEM_HD = D_MODEL // MEM_HEADS
D_FF = 4 * D_MODEL
EPS = 1e-6
IN_SIZES = (NSA_HEADS * HEAD_DIM, 6 * NSA_KV_HEADS * HEAD_DIM, 3 * NSA_HEADS,
            GLA_HEADS * GLA_DK, GLA_HEADS * GLA_DK, GLA_HEADS * GLA_DV, GLA_RANK, GLA_HEADS * GLA_DV,
            RET_HEADS * RET_DIM, RET_HEADS * RET_DIM, RET_HEADS * RET_DIM, RET_HEADS * RET_DIM)

LANES = 128
VMEM_LIMIT = 56 << 20
NEG = -1e30
M_INIT = -1e29
KV_W = 2 * NSA_KV_HEADS * HEAD_DIM

C_Q = 0
C_KV = 512
C_GLA = 1280
C_RET = 2048
C_TAIL = 3072
W_IN_COLS = 3200
GLA_W = 896
RET_W = 1024


def _cparams(sem, vmem=VMEM_LIMIT):
    return pltpu.CompilerParams(dimension_semantics=sem, vmem_limit_bytes=vmem)


def _dot(a, b):
    return jnp.dot(a, b, preferred_element_type=F32)


def _dot_nt(a, b):
    return lax.dot_general(a, b, (((1,), (1,)), ((), ())), preferred_element_type=F32)


def _dot_tn(a, b):
    return lax.dot_general(a, b, (((0,), (0,)), ((), ())), preferred_element_type=F32)


def _split3(x):
    hi = x.astype(BF16)
    r = x - hi.astype(F32)
    mid = r.astype(BF16)
    lo = (r - mid.astype(F32)).astype(BF16)
    return hi, mid, lo


def _rms(x, eps=EPS):
    return x * lax.rsqrt(jnp.mean(x * x, axis=-1, keepdims=True) + eps)


def _silu(x):
    return x * jax.nn.sigmoid(x)


def _rope128(v, c, sa, sb, half):
    return v * c + pltpu.roll(v, LANES - half, 1) * sa + pltpu.roll(v, half, 1) * sb


def _proj_kernel(x_ref, gain_ref, w_ref, wa2_ref, ba_ref, nc_ref, nsa_ref, nsb_ref, rc_ref, rsa_ref, rsb_ref,
                 q_ref, cmp_ref, slc_ref, win_ref, gate_ref, gla_ref, ret_ref, *, valid_period):
    x = x_ref[...]
    xn = _rms(x) * gain_ref[...]
    if valid_period is not None:
        period, n_valid = valid_period
        row = lax.broadcasted_iota(jnp.int32, (x.shape[0], 1), 0)
        xn = jnp.where(lax.rem(row, period) < n_valid, xn, 0.0)
    xb = xn.astype(BF16)

    def mm(a, b):
        return _dot(xb, w_ref[:, a:b])

    nc, nsa, nsb = nc_ref[...], nsa_ref[...], nsb_ref[...]
    rc, rsa, rsb = rc_ref[...], rsa_ref[...], rsb_ref[...]
    half_n = ROT_DIM // 2
    half_r = RET_DIM // 2

    q = mm(C_Q, C_KV)
    for c in range(4):
        sl = slice(c * LANES, (c + 1) * LANES)
        q_ref[:, sl] = (_rope128(q[:, sl], nc, nsa, nsb, half_n) * (HEAD_DIM ** -0.5)).astype(q_ref.dtype)

    kv = mm(C_KV, C_GLA)
    for br, ref in enumerate((cmp_ref, slc_ref, win_ref)):
        ref[:, 0:LANES] = _rope128(kv[:, br * KV_W:br * KV_W + LANES], nc, nsa, nsb, half_n)
        ref[:, LANES:KV_W] = kv[:, br * KV_W + LANES:(br + 1) * KV_W]

    tail = mm(C_TAIL, W_IN_COLS)
    gate_ref[...] = jax.nn.sigmoid(tail)
    z = jnp.dot(tail, wa2_ref[...], preferred_element_type=F32, precision=lax.Precision.HIGHEST) + ba_ref[...]
    log_a = (jnp.minimum(z, 0.0) - jnp.log1p(jnp.exp(-jnp.abs(z)))) * (1.0 / GLA_TAU)

    gl = mm(C_GLA, C_RET)
    gla_ref[:, 0:128] = gl[:, 0:128] * (GLA_DK ** -0.5)
    gla_ref[:, 128:256] = gl[:, 128:256]
    gla_ref[:, 256:384] = log_a
    gla_ref[:, 384:640] = gl[:, 256:512]
    gla_ref[:, 640:896] = _silu(gl[:, 512:768])

    rt = mm(C_RET, C_TAIL)
    for c in range(2):
        sl = slice(c * LANES, (c + 1) * LANES)
        ret_ref[:, sl] = _rope128(rt[:, sl], rc, rsa, rsb, half_r)
        sk = slice(256 + c * LANES, 256 + (c + 1) * LANES)
        ret_ref[:, sk] = _rope128(rt[:, sk], rc, rsa, rsb, half_r) * (RET_DIM ** -0.5)
    ret_ref[:, 512:768] = rt[:, 512:768]
    ret_ref[:, 768:1024] = _silu(rt[:, 768:1024])


def _project(x, gain, w, wa2, ba, tabs, *, tm, valid_period=None):
    n = x.shape[0]
    p = tabs[0].shape[0]
    nt = p // tm
    row = lambda w_: pl.BlockSpec((tm, w_), lambda i: (i, 0))
    full = lambda a: pl.BlockSpec(a.shape, lambda i: (0, 0))
    tab = pl.BlockSpec((tm, LANES), lambda i: (i % nt, 0))
    out_shape = (jax.ShapeDtypeStruct((n, 512), BF16),
                 jax.ShapeDtypeStruct((n, KV_W), F32), jax.ShapeDtypeStruct((n, KV_W), F32),
                 jax.ShapeDtypeStruct((n, KV_W), F32), jax.ShapeDtypeStruct((n, LANES), F32),
                 jax.ShapeDtypeStruct((n, GLA_W), F32), jax.ShapeDtypeStruct((n, RET_W), F32))
    return pl.pallas_call(
        functools.partial(_proj_kernel, valid_period=valid_period),
        out_shape=out_shape,
        grid=(n // tm,),
        in_specs=[row(D_MODEL), full(gain), full(w), full(wa2), full(ba)] + [tab] * 6,
        out_specs=(row(512), row(KV_W), row(KV_W), row(KV_W), row(LANES), row(GLA_W), row(RET_W)),
        compiler_params=_cparams(("parallel",)),
        name="proj",
    )(x, gain, w, wa2, ba, *tabs)


def _rope_tables(pos, inv_freq):
    nf = inv_freq.shape[0]
    ang = pos.astype(F32)[:, None] * inv_freq[None, :]
    cos, sin = jnp.cos(ang), jnp.sin(ang)
    ones = jnp.ones((pos.shape[0], HEAD_DIM - 2 * nf), F32)
    zeros = jnp.zeros((pos.shape[0], HEAD_DIM - nf), F32)
    c = jnp.concatenate([cos, cos, ones], axis=1)
    sa = jnp.concatenate([-sin, zeros], axis=1)
    sb = jnp.concatenate([zeros[:, :nf], sin, zeros[:, :HEAD_DIM - 2 * nf]], axis=1)
    return tuple(jnp.tile(t, (1, LANES // HEAD_DIM)) for t in (c, sa, sb))


def _all_rope_tables(pos):
    nsa_f = ROPE_THETA ** (-jnp.arange(0, ROT_DIM, 2, dtype=F32) / ROT_DIM)
    ret_f = RET_THETA ** (-jnp.linspace(0.0, 1.0, RET_DIM // 2, dtype=F32))
    return _rope_tables(pos, nsa_f) + _rope_tables(pos, ret_f)


def _pack_w_in(w):
    offs = np.cumsum((0,) + IN_SIZES)
    nq, nkv, ngate, gq, gk, gv, ga, gr, rq, rk, rv, rg = [w[:, offs[i]:offs[i + 1]] for i in range(12)]
    pad = jnp.zeros((w.shape[0], W_IN_COLS - C_TAIL - ngate.shape[1] - ga.shape[1]), w.dtype)
    return jnp.concatenate([nq, nkv, gq, gk, gv, gr, rq, rk, rv, rg, ngate, ga, pad], axis=1).astype(BF16)


def _pack_wa2(wa2):
    top = jnp.zeros((3 * NSA_HEADS, wa2.shape[1]), wa2.dtype)
    bot = jnp.zeros((LANES - 3 * NSA_HEADS - GLA_RANK, wa2.shape[1]), wa2.dtype)
    return jnp.concatenate([top, wa2, bot], axis=0)


def _gelu_tanh(x):
    return 0.5 * x * (1.0 + jnp.tanh(0.7978845608028654 * (x + 0.044715 * (x * x * x))))


def _compress_kernel(x_ref, pe_ref, w1_ref, w2_ref, o_ref):
    xb = (x_ref[...] + pe_ref[...]).astype(BF16)
    h = _gelu_tanh(_dot(xb, w1_ref[...]))
    o_ref[...] = _dot(h.astype(BF16), w2_ref[...])


def _compress(rows_flat, pe_big, w1_big, w2_big, *, tr):
    r, k = rows_flat.shape
    tr = min(tr, r)
    full = lambda a: pl.BlockSpec(a.shape, lambda i: (0, 0))
    return pl.pallas_call(
        _compress_kernel,
        out_shape=jax.ShapeDtypeStruct((r, KV_W), F32),
        grid=(r // tr,),
        in_specs=[pl.BlockSpec((tr, k), lambda i: (i, 0)), full(pe_big), full(w1_big), full(w2_big)],
        out_specs=pl.BlockSpec((tr, KV_W), lambda i: (i, 0)),
        compiler_params=_cparams(("parallel",)),
        name="compress",
    )(rows_flat, pe_big, w1_big, w2_big)


def _pack_compress(pe, w1, w2):
    g, d = NSA_KV_HEADS, HEAD_DIM
    hid = w1.shape[-1]
    pe_big = jnp.broadcast_to(pe.transpose(1, 0, 2)[:, :, None, :], (CMP_BLOCK, 2, g, d)).reshape(1, -1)
    eye = jnp.eye(2 * g, dtype=w1.dtype).reshape(2, g, 2, g)
    w1r = w1.reshape(2, CMP_BLOCK, d, hid)
    w1_big = jnp.einsum('kjdc,kgKG->jkgdKGc', w1r, eye).reshape(CMP_BLOCK * KV_W, 2 * g * hid)
    w2_big = jnp.einsum('kce,kgKG->kgcKGe', w2, eye).reshape(2 * g * hid, KV_W)
    return pe_big, w1_big.astype(BF16), w2_big.astype(BF16)


def _stack_heads(q, g):
    return jnp.concatenate([q[:, (g * NSA_REP + r) * HEAD_DIM:(g * NSA_REP + r + 1) * HEAD_DIM]
                            for r in range(NSA_REP)], axis=0)


def _flash_t(kv_ref, chains, tpos, lo, hi, tk):
    nq = tpos.shape[1]

    def body(kt, carry):
        k0 = pl.multiple_of(kt * tk, tk)
        kpos = k0 + lax.broadcasted_iota(jnp.int32, (tk, 1), 0)
        out = []
        for (kcol, vcol, q_t, mask_fn), (m_old, l_old, acc) in zip(chains, carry):
            kb = kv_ref[pl.ds(k0, tk), kcol].astype(BF16)
            vb = kv_ref[pl.ds(k0, tk), vcol].astype(BF16)
            s_t = mask_fn(_dot(kb, q_t), k0, kpos, tpos)
            m_new = jnp.maximum(m_old, jnp.max(s_t, axis=0, keepdims=True))
            alpha = jnp.exp(m_old - m_new)
            p = jnp.exp(s_t - m_new)
            l_new = alpha * l_old + jnp.sum(p, axis=0, keepdims=True)
            out.append((m_new, l_new, alpha * acc + _dot_tn(vb, p.astype(BF16))))
        return tuple(out)

    init = tuple((jnp.full((1, nq), M_INIT, F32), jnp.zeros((1, nq), F32), jnp.zeros((HEAD_DIM, nq), F32))
                 for _ in chains)
    return [acc / l_fin for _, l_fin, acc in lax.fori_loop(lo, hi, body, init)]


def _nsa_prompt_kernel(q_ref, kcv_ref, slc_ref, win_ref, gate_ref, exp_ref, o_ref, selx_ref, *, tq, tk):
    i = pl.program_id(1)
    mq = NSA_REP * tq
    nb = kcv_ref.shape[0]
    nsb = nb // 2
    q_t = q_ref[...].astype(F32).T
    gate_t = gate_ref[...].T
    tcol = i * tq + lax.rem(lax.broadcasted_iota(jnp.int32, (1, mq), 1), tq)
    tq_col = i * tq + lax.broadcasted_iota(jnp.int32, (1, tq), 1)
    blk_row = lax.broadcasted_iota(jnp.int32, (nb, 1), 0)
    blk = jnp.where(blk_row < nsb, 2 * blk_row, 2 * (blk_row - nsb) + 1)
    cmask = (blk + 1) * CMP_BLOCK - 1 <= tcol
    jrow = lax.broadcasted_iota(jnp.int32, (nsb, 1), 0)
    cur = tq_col // SEL_BLOCK
    forced = (jrow == 0) | (jrow == cur) | (jrow == cur - 1)
    valid = jrow * SEL_BLOCK <= tq_col
    n_kt = ((i + 1) * tq + tk - 1) // tk
    win_lo = jnp.maximum(i * tq - WINDOW + 1, 0) // tk

    def sel_mask(g):
        def fn(s, k0, kpos, tpos):
            sx = selx_ref[g, pl.ds(k0, tk), :]
            s = jnp.where(jnp.concatenate([sx] * NSA_REP, axis=1) > 0.5, s, NEG)
            return jnp.where(kpos <= tpos, s, NEG)
        return fn

    def win_mask(s, k0, kpos, tpos):
        return jnp.where(kpos <= tpos, jnp.where(kpos > tpos - WINDOW, s, NEG), NEG)

    kcols = [slice(g * HEAD_DIM, (g + 1) * HEAD_DIM) for g in range(NSA_KV_HEADS)]
    vcols = [slice((NSA_KV_HEADS + g) * HEAD_DIM, (NSA_KV_HEADS + g + 1) * HEAD_DIM) for g in range(NSA_KV_HEADS)]
    qgs, o_cmps = [], []
    for g in range(NSA_KV_HEADS):
        qg = jnp.concatenate([q_t[(g * NSA_REP + r) * HEAD_DIM:(g * NSA_REP + r + 1) * HEAD_DIM, :]
                              for r in range(NSA_REP)], axis=1).astype(BF16)
        qgs.append(qg)
        kcol, vcol = kcols[g], vcols[g]

        s_t = _dot(kcv_ref[:, kcol].astype(BF16), qg)
        sm = jnp.where(cmask, s_t, NEG)
        mx = jnp.max(sm, axis=0, keepdims=True)
        mx = jnp.where(mx > 0.5 * NEG, mx, 0.0)
        e = jnp.where(cmask, jnp.exp(sm - mx), 0.0)
        den = jnp.sum(e, axis=0, keepdims=True)
        p = e / jnp.where(den > 0.0, den, 1.0)
        o_cmp = _dot_tn(kcv_ref[:, vcol].astype(BF16), p.astype(BF16))
        ps = p[:, 0:tq]
        for r in range(1, NSA_REP):
            ps = ps + p[:, r * tq:(r + 1) * tq]
        imp = ps[0:nsb] + ps[nsb:nb]

        score = jnp.where(forced, -NEG, jnp.where(valid, imp, NEG))
        rank = jnp.zeros((nsb, tq), F32)
        for c in range(nsb):
            sc = score[c:c + 1, :]
            beats = (sc > score) | ((sc == score) & (jrow > c))
            rank = rank + jnp.where(beats, 1.0, 0.0)
        sel = jnp.where(rank < float(N_SELECT), 1.0, 0.0).astype(BF16)
        selx_ref[g] = _dot(exp_ref[...], sel)
        o_cmps.append(o_cmp)

    groups = range(NSA_KV_HEADS)
    o_sels = _flash_t(slc_ref, [(kcols[g], vcols[g], qgs[g], sel_mask(g)) for g in groups], tcol, 0, n_kt, tk)
    o_wins = _flash_t(win_ref, [(kcols[g], vcols[g], qgs[g], win_mask) for g in groups], tcol, win_lo, n_kt, tk)

    for g in groups:
        outs = []
        for r in range(NSA_REP):
            h = g * NSA_REP + r
            cs = slice(r * tq, (r + 1) * tq)
            outs.append(o_cmps[g][:, cs] * gate_t[3 * h:3 * h + 1, :]
                        + o_sels[g][:, cs] * gate_t[3 * h + 1:3 * h + 2, :]
                        + o_wins[g][:, cs] * gate_t[3 * h + 2:3 * h + 3, :])
        o_ref[:, g * NSA_REP * HEAD_DIM:(g + 1) * NSA_REP * HEAD_DIM] = jnp.concatenate(outs, axis=0).T


def _nsa_prompt(q, kcv, slc, win, gates, *, b, t, tq, tk):
    n = b * t
    nq = t // tq
    nb = t // CMP_BLOCK
    nsb = t // SEL_BLOCK
    tk = min(tk, t)
    expand = (np.arange(t)[:, None] // SEL_BLOCK == np.arange(nsb)[None, :]).astype(np.float32)
    expand = jnp.asarray(expand, BF16)
    return pl.pallas_call(
        functools.partial(_nsa_prompt_kernel, tq=tq, tk=tk),
        out_shape=jax.ShapeDtypeStruct((n, NSA_HEADS * HEAD_DIM), F32),
        grid=(b, nq),
        in_specs=[pl.BlockSpec((tq, 512), lambda bb, i: (bb * nq + i, 0)),
                  pl.BlockSpec((nb, KV_W), lambda bb, i: (bb, 0)),
                  pl.BlockSpec((t, KV_W), lambda bb, i: (bb, 0)),
                  pl.BlockSpec((t, KV_W), lambda bb, i: (bb, 0)),
                  pl.BlockSpec((tq, LANES), lambda bb, i: (bb * nq + i, 0)),
                  pl.BlockSpec((t, nsb), lambda bb, i: (0, 0))],
        out_specs=pl.BlockSpec((tq, 512), lambda bb, i: (bb * nq + i, 0)),
        scratch_shapes=[pltpu.VMEM((NSA_KV_HEADS, t, tq), F32)],
        compiler_params=_cparams(("parallel", "arbitrary")),
        name="nsa_prompt",
    )(q, kcv, slc, win, gates, expand)


def _gla_level_matrix(tt):
    lv = int(math.log2(tt))
    m = np.zeros((lv, tt, tt), np.float32)
    t = np.arange(tt)
    for l in range(lv):
        half = 1 << l
        split = ((t >> (l + 1)) << (l + 1)) + half - 1
        u = np.arange(tt)[None, :]
        upper = t > split
        m[l] = np.where(upper[:, None], (u > split[:, None]) & (u <= t[:, None]),
                        (u > t[:, None]) & (u <= split[:, None]))
    return m.reshape(lv * tt, tt)


def _ret_log_decay():
    return [float(np.log(np.float32(1.0) - np.float32(2.0) ** np.float32(-5.0 - h))) for h in range(RET_HEADS)]


def _recur_kernel(gla_ref, ret_ref, gn_ref, lvl_ref, tril_ref, sg0_ref, sr0_ref,
                  o_ref, sg_out_ref, sr_out_ref, sg_ref, sr_ref, *, tt, n_valid):
    ti = pl.program_id(1)
    nt = pl.num_programs(1)
    levels = int(math.log2(tt))

    @pl.when(ti == 0)
    def _():
        sg_ref[...] = sg0_ref[...]
        sr_ref[...] = sr0_ref[...]

    rowi = lax.broadcasted_iota(jnp.int32, (tt, 1), 0)
    coli = lax.broadcasted_iota(jnp.int32, (1, tt), 1)

    gq = gla_ref[:, 0:128]
    gk = gla_ref[:, 128:256]
    la = gla_ref[:, 256:384]
    gv = gla_ref[:, 384:640]
    gr = gla_ref[:, 640:896]
    if n_valid < tt:
        la = jnp.where(rowi < n_valid, la, 0.0)
    parts = _split3(la)
    tril = tril_ref[...]
    lvl = lvl_ref[...]
    ones = jnp.ones((tt, GLA_DV), BF16)
    cum = sum(_dot(tril, pt) for pt in parts)
    dlv = sum(_dot(lvl, pt) for pt in parts)
    tot = sum(_dot_tn(pt, ones) for pt in parts)
    q_dec = gq * jnp.exp(cum)
    k_dec = gk * jnp.exp(cum[tt - 1:tt, :] - cum)
    gla_out = []
    for h in range(GLA_HEADS):
        sl = slice(h * GLA_DK, (h + 1) * GLA_DK)
        vs = slice(h * GLA_DV, (h + 1) * GLA_DV)
        qh, kh = gq[:, sl], gk[:, sl]
        vh = gv[:, vs].astype(BF16)
        attn = jnp.where(rowi == coli, _dot_nt(qh.astype(BF16), kh.astype(BF16)), 0.0)
        for l in range(levels):
            ed = jnp.exp(dlv[l * tt:(l + 1) * tt, sl])
            upper = ((rowi >> l) & 1) == 1
            qe = jnp.where(upper, qh * ed, 0.0).astype(BF16)
            ke = jnp.where(upper, 0.0, kh * ed).astype(BF16)
            same = (rowi >> (l + 1)) == (coli >> (l + 1))
            attn = attn + jnp.where(same, _dot_nt(qe, ke), 0.0)
        s_h = sg_ref[h]
        o = _dot(attn.astype(BF16), vh) + _dot(q_dec[:, sl].astype(BF16), s_h.astype(BF16))
        sg_ref[h] = jnp.exp(tot[sl, :]) * s_h + _dot_tn(k_dec[:, sl].astype(BF16), vh)
        gla_out.append(_rms(o) * gn_ref[...] * gr[:, vs])
    o_ref[:, 0:256] = jnp.concatenate(gla_out, axis=1)

    rel = (rowi - coli).astype(F32)
    pos1 = (rowi + 1).astype(F32)
    left = (n_valid - 1 - rowi).astype(F32)
    ret_out = []
    for h, lg in enumerate(_ret_log_decay()):
        sl = slice(h * RET_DIM, (h + 1) * RET_DIM)
        qh = ret_ref[:, sl]
        kh = ret_ref[:, 256 + h * RET_DIM:256 + (h + 1) * RET_DIM]
        vh = ret_ref[:, 512 + h * RET_DIM:512 + (h + 1) * RET_DIM].astype(BF16)
        gh = ret_ref[:, 768 + h * RET_DIM:768 + (h + 1) * RET_DIM]
        dmat = jnp.where(rel >= 0.0, jnp.exp(lg * jnp.maximum(rel, 0.0)), 0.0)
        attn = _dot_nt(qh.astype(BF16), kh.astype(BF16)) * dmat
        s_h = sr_ref[h]
        o = _dot(attn.astype(BF16), vh) + _dot(qh.astype(BF16), s_h.astype(BF16)) * jnp.exp(lg * pos1)
        sr_ref[h] = math.exp(lg * n_valid) * s_h + _dot_tn((kh * jnp.exp(lg * left)).astype(BF16), vh)
        ret_out.append(_rms(o) * gh)
    o_ref[:, 256:512] = jnp.concatenate(ret_out, axis=1)

    @pl.when(ti == nt - 1)
    def _():
        sg_out_ref[...] = sg_ref[...]
        sr_out_ref[...] = sr_ref[...]


def _recurrent(gla, ret, gnorm, sg0, sr0, *, b, t, tt, n_valid=None):
    n = b * t
    nt = t // tt
    n_valid = tt if n_valid is None else n_valid
    assert n_valid == tt or nt == 1
    lvl = jnp.asarray(_gla_level_matrix(tt), BF16)
    tril = jnp.asarray(np.tril(np.ones((tt, tt), np.float32)), BF16)
    full2 = lambda a: pl.BlockSpec(a.shape, lambda bb, i: (0, 0))
    st = lambda a: pl.BlockSpec((None,) + a.shape[1:], lambda bb, i: (bb, 0, 0, 0))
    return pl.pallas_call(
        functools.partial(_recur_kernel, tt=tt, n_valid=n_valid),
        out_shape=(jax.ShapeDtypeStruct((n, 512), F32), jax.ShapeDtypeStruct(sg0.shape, F32),
                   jax.ShapeDtypeStruct(sr0.shape, F32)),
        grid=(b, nt),
        in_specs=[pl.BlockSpec((tt, GLA_W), lambda bb, i: (bb * nt + i, 0)),
                  pl.BlockSpec((tt, RET_W), lambda bb, i: (bb * nt + i, 0)),
                  full2(gnorm), full2(lvl), full2(tril), st(sg0), st(sr0)],
        out_specs=(pl.BlockSpec((tt, 512), lambda bb, i: (bb * nt + i, 0)), st(sg0), st(sr0)),
        scratch_shapes=[pltpu.VMEM(sg0.shape[1:], F32), pltpu.VMEM(sr0.shape[1:], F32)],
        compiler_params=_cparams(("parallel", "arbitrary")),
        name="recurrent",
    )(gla, ret, gnorm, lvl, tril, sg0, sr0)


def _outproj_kernel(x_ref, a_ref, b_ref, w_ref, o_ref):
    ka = a_ref.shape[1]
    o_ref[...] = (x_ref[...] + _dot(a_ref[...].astype(BF16), w_ref[0:ka, :])
                  + _dot(b_ref[...].astype(BF16), w_ref[ka:, :]))


def _outproj(x, a, b_, w, *, tm):
    n = x.shape[0]
    row = lambda a_: pl.BlockSpec((tm, a_.shape[1]), lambda i: (i, 0))
    return pl.pallas_call(
        _outproj_kernel,
        out_shape=jax.ShapeDtypeStruct(x.shape, F32),
        grid=(n // tm,),
        in_specs=[row(x), row(a), row(b_), pl.BlockSpec(w.shape, lambda i: (0, 0))],
        out_specs=row(x),
        compiler_params=_cparams(("parallel",)),
        name="outproj",
    )(x, a, b_, w)


def _cross_kernel(x_ref, gain_ref, wq_ref, wo_ref, kv_ref, o_ref):
    x = x_ref[...]
    xb = (_rms(x) * gain_ref[...]).astype(BF16)
    q = _dot(xb, wq_ref[...])
    heads = []
    for h in range(MEM_HEADS):
        qh = q[:, h * MEM_HD:(h + 1) * MEM_HD].astype(BF16)
        kh = kv_ref[:, h * MEM_HD:(h + 1) * MEM_HD].astype(BF16)
        vh = kv_ref[:, (MEM_HEADS + h) * MEM_HD:(MEM_HEADS + h + 1) * MEM_HD].astype(BF16)
        s = _dot_nt(qh, kh) * (MEM_HD ** -0.5)
        e = jnp.exp(s - jnp.max(s, axis=-1, keepdims=True))
        p = e / jnp.sum(e, axis=-1, keepdims=True)
        heads.append(_dot(p.astype(BF16), vh))
    att = jnp.concatenate(heads, axis=1).astype(BF16)
    o_ref[...] = x + _dot(att, wo_ref[...])


def _cross(x, gain, wq, wo, memkv, *, b, t, tq):
    nq = t // tq
    n_mem = memkv.shape[0] // b
    full = lambda a: pl.BlockSpec(a.shape, lambda bb, i: (0, 0))
    row = pl.BlockSpec((tq, D_MODEL), lambda bb, i: (bb * nq + i, 0))
    return pl.pallas_call(
        _cross_kernel,
        out_shape=jax.ShapeDtypeStruct(x.shape, F32),
        grid=(b, nq),
        in_specs=[row, full(gain), full(wq), full(wo),
                  pl.BlockSpec((n_mem, memkv.shape[1]), lambda bb, i: (bb, 0))],
        out_specs=row,
        compiler_params=_cparams(("parallel", "arbitrary")),
        name="cross",
    )(x, gain, wq, wo, memkv)


def _ffn_kernel(x_ref, gain_ref, w1_ref, w2_ref, gf_ref, o_ref, xn_ref, acc_ref, *, final_norm):
    j = pl.program_id(1)

    @pl.when(j == 0)
    def _():
        xn_ref[...] = (_rms(x_ref[...]) * gain_ref[...]).astype(BF16)
        acc_ref[...] = x_ref[...]

    h = jnp.maximum(_dot(xn_ref[...], w1_ref[...]), 0.0)
    acc_ref[...] += _dot((h * h).astype(BF16), w2_ref[...])

    @pl.when(j == pl.num_programs(1) - 1)
    def _():
        y = acc_ref[...]
        if final_norm:
            y = _rms(y) * gf_ref[...]
        o_ref[...] = y


def _ffn(x, gain, w1, w2, gfinal, *, tm, tf, final_norm):
    n = x.shape[0]
    row = pl.BlockSpec((tm, D_MODEL), lambda i, j: (i, 0))
    vec = pl.BlockSpec((1, D_MODEL), lambda i, j: (0, 0))
    return pl.pallas_call(
        functools.partial(_ffn_kernel, final_norm=final_norm),
        out_shape=jax.ShapeDtypeStruct(x.shape, F32),
        grid=(n // tm, D_FF // tf),
        in_specs=[row, vec, pl.BlockSpec((D_MODEL, tf), lambda i, j: (0, j)),
                  pl.BlockSpec((tf, D_MODEL), lambda i, j: (j, 0)), vec],
        out_specs=row,
        scratch_shapes=[pltpu.VMEM((tm, D_MODEL), BF16), pltpu.VMEM((tm, D_MODEL), F32)],
        compiler_params=_cparams(("parallel", "arbitrary")),
        name="ffn",
    )(x, gain, w1, w2, gfinal)


def _matmul_kernel(x_ref, w_ref, o_ref):
    o_ref[...] = _dot(x_ref[...].astype(BF16), w_ref[...])


def _matmul(x, w, *, tm):
    n, k = x.shape
    return pl.pallas_call(
        _matmul_kernel,
        out_shape=jax.ShapeDtypeStruct((n, w.shape[1]), F32),
        grid=(n // tm,),
        in_specs=[pl.BlockSpec((tm, k), lambda i: (i, 0)), pl.BlockSpec(w.shape, lambda i: (0, 0))],
        out_specs=pl.BlockSpec((tm, w.shape[1]), lambda i: (i, 0)),
        compiler_params=_cparams(("parallel",)),
        name="memkv",
    )(x, w)


TS = 8


def _pool_slabs(pool):
    depth, n_pool = pool.shape[:2]
    return jnp.transpose(pool, (0, 1, 3, 4, 5, 2)).reshape(depth * n_pool, KV_W, pool.shape[2])


def _cmp_pages_kernel(pt_ref, pool_ref, wexp_hbm, pe_ref, w1_ref, w2_ref, o_ref, buf_ref, wbuf_ref, sem_ref,
                      *, pg, base):
    s = pl.program_id(0)
    ns = pl.num_programs(0)
    slab = pool_ref.shape[1]
    n_pairs = wbuf_ref.shape[1]
    blocks_per_page = PAGE_SIZE // CMP_BLOCK

    def page_copy(page, slot, j):
        return pltpu.make_async_copy(pool_ref.at[page], buf_ref.at[slot, pl.ds(j * slab, slab), :], sem_ref.at[slot])

    def fetch(step, slot):
        def body(j, c):
            page_copy(base + pt_ref[step * pg + j], slot, j).start()
            return c
        lax.fori_loop(0, pg, body, 0)

    @pl.when(s == 0)
    def _():
        weights = pltpu.make_async_copy(wexp_hbm, wbuf_ref, sem_ref.at[2])
        weights.start()
        fetch(0, 0)
        weights.wait()

    @pl.when(s + 1 < ns)
    def _():
        fetch(s + 1, lax.rem(s + 1, 2))

    slot = lax.rem(s, 2)

    def wait_body(j, c):
        page_copy(0, slot, j).wait()
        return c
    lax.fori_loop(0, pg, wait_body, 0)

    for kv in range(2):
        pew = sum(_dot(part, w1_ref[kv]) for part in _split3(pe_ref[kv]))[0:1]
        pew = jnp.concatenate([pew] * blocks_per_page, axis=1)
        for g in range(NSA_KV_HEADS):
            kvg = kv * NSA_KV_HEADS + g

            def body(dd, acc):
                r0 = kvg * HEAD_DIM + 2 * dd
                x0 = buf_ref[slot, pl.ds(r0, pg, stride=slab), :]
                x1 = buf_ref[slot, pl.ds(r0 + 1, pg, stride=slab), :]
                x = jnp.concatenate([x0, x1], axis=1).astype(BF16)
                return acc + _dot(x, wbuf_ref[kv, dd])
            acc = lax.fori_loop(0, n_pairs, body, jnp.zeros((pg, w2_ref.shape[1]), F32), unroll=8)
            width = w2_ref.shape[2]
            o_ref[:, kvg * width:(kvg + 1) * width] = _dot(_gelu_tanh(acc + pew).astype(BF16), w2_ref[kv])


def _pack_compress_pages(pe, w1, w2):
    d = HEAD_DIM
    nblk = PAGE_SIZE // CMP_BLOCK
    hid = w1.shape[-1]
    eye = jnp.eye(nblk, dtype=w1.dtype)
    w1r = w1.reshape(2, CMP_BLOCK, d, hid)
    wexp = jnp.einsum('kjdc,nm->kdnjmc', w1r, eye).reshape(2, d // 2, 2 * PAGE_SIZE, nblk * hid)
    w2bd = jnp.einsum('kce,nm->kncme', w2, eye).reshape(2, nblk * hid, nblk * w2.shape[-1])
    pe_rows = jnp.pad(pe.reshape(2, 1, CMP_BLOCK * d), ((0, 0), (0, 7), (0, 0)))
    return wexp.astype(BF16), pe_rows, w1.astype(BF16), w2bd.astype(BF16)


def _compress_pages(page_table_flat, pool_slabs, wexp, pe_rows, w1b, w2bd, *, base, pg):
    n_pages = page_table_flat.shape[0]
    slab = pool_slabs.shape[1]
    width = w2bd.shape[2]
    full3 = lambda a: pl.BlockSpec(a.shape, lambda i, pt: (0, 0, 0))
    return pl.pallas_call(
        functools.partial(_cmp_pages_kernel, pg=pg, base=base),
        out_shape=jax.ShapeDtypeStruct((n_pages, 2 * NSA_KV_HEADS * width), F32),
        grid_spec=pltpu.PrefetchScalarGridSpec(
            num_scalar_prefetch=1,
            grid=(n_pages // pg,),
            in_specs=[pl.BlockSpec(memory_space=pl.ANY), pl.BlockSpec(memory_space=pl.ANY),
                      full3(pe_rows), full3(w1b), full3(w2bd)],
            out_specs=pl.BlockSpec((pg, 2 * NSA_KV_HEADS * width), lambda i, pt: (i, 0)),
            scratch_shapes=[pltpu.VMEM((2, pg * slab, pool_slabs.shape[2]), F32), pltpu.VMEM(wexp.shape, BF16),
                            pltpu.SemaphoreType.DMA((3,))]),
        compiler_params=_cparams(("arbitrary",)),
        name="compress_pages",
    )(page_table_flat, pool_slabs, wexp, pe_rows, w1b, w2bd)


def _heads_to_lanes(o, ts):
    return jnp.concatenate([o[r * ts:(r + 1) * ts] for r in range(NSA_REP)], axis=1)


def _nsa_sample_a_kernel(q_ref, kcp_ref, kcn_ref, wc_ref, wn_ref, ocmp_ref, owin_ref, idx_ref, wout_ref,
                         *, past, n_new_blk, tv):
    ts = q_ref.shape[0]
    q = q_ref[...].astype(F32)
    hp = kcp_ref.shape[0]
    bpp = PAGE_SIZE // CMP_BLOCK
    plane_w = bpp * HEAD_DIM
    nbp = hp * bpp
    nsb_past = nbp * CMP_BLOCK // SEL_BLOCK
    nsb = nsb_past + (n_new_blk + 1) // 2
    wb = wc_ref.shape[1]
    m_rows = NSA_REP * ts
    row = lax.broadcasted_iota(jnp.int32, (m_rows, 1), 0)
    tpos = past + lax.rem(row, ts)
    t8 = past + lax.broadcasted_iota(jnp.int32, (ts, 1), 0)
    k_new = kcn_ref[...]
    w_old = wc_ref[...]
    w_new = wn_ref[...]
    nn = k_new.shape[0]
    pagei = lax.broadcasted_iota(jnp.int32, (1, hp), 1)
    cn = lax.broadcasted_iota(jnp.int32, (1, nn), 1)
    mask_p = jnp.concatenate([(bpp * pagei + n + 1) * CMP_BLOCK - 1 <= tpos for n in range(bpp)], axis=1)
    mask_n = ((nbp + cn + 1) * CMP_BLOCK - 1 <= tpos) & (cn < n_new_blk)
    kp_old = past - wb + lax.broadcasted_iota(jnp.int32, (1, wb), 1)
    kp_new = past + lax.broadcasted_iota(jnp.int32, (1, w_new.shape[0]), 1)
    wmask_old = (kp_old <= tpos) & (kp_old > tpos - WINDOW) & (kp_old >= 0)
    wmask_new = (kp_new <= tpos) & (kp_new > tpos - WINDOW)
    lane = lax.broadcasted_iota(jnp.int32, (1, LANES), 1)
    j = jnp.concatenate([2 * pagei, 2 * pagei + 1, nsb_past + lane], axis=1)
    width = 2 * hp + LANES

    new_t = jnp.concatenate([w_new, jnp.zeros((LANES - w_new.shape[0], w_new.shape[1]), F32)], axis=0).T
    wout_ref[...] = jnp.concatenate([w_old[:, tv:], new_t[:, 0:tv]], axis=1)

    def joint_softmax(parts):
        mx = None
        for s, mk in parts:
            cur = jnp.max(jnp.where(mk, s, NEG), axis=-1, keepdims=True)
            mx = cur if mx is None else jnp.maximum(mx, cur)
        mx = jnp.where(mx > 0.5 * NEG, mx, 0.0)
        es = [jnp.where(mk, jnp.exp(jnp.where(mk, s, NEG) - mx), 0.0) for s, mk in parts]
        den = sum(jnp.sum(e, axis=-1, keepdims=True) for e in es)
        inv = 1.0 / jnp.where(den > 0.0, den, 1.0)
        return [e * inv for e in es]

    def fold_heads(p):
        out = p[0:ts]
        for r in range(1, NSA_REP):
            out = out + p[r * ts:(r + 1) * ts]
        return out

    for g in range(NSA_KV_HEADS):
        qg = _stack_heads(q, g).astype(BF16)
        kcol = slice(g * HEAD_DIM, (g + 1) * HEAD_DIM)
        vcol = slice((NSA_KV_HEADS + g) * HEAD_DIM, (NSA_KV_HEADS + g + 1) * HEAD_DIM)

        kplanes = [kcp_ref[:, g * plane_w + n * HEAD_DIM:g * plane_w + (n + 1) * HEAD_DIM].astype(BF16)
                   for n in range(bpp)]
        vplanes = [kcp_ref[:, (NSA_KV_HEADS + g) * plane_w + n * HEAD_DIM:
                           (NSA_KV_HEADS + g) * plane_w + (n + 1) * HEAD_DIM].astype(BF16) for n in range(bpp)]
        s_past = jnp.concatenate([_dot_nt(qg, kp) for kp in kplanes], axis=1)
        p_p, p_n = joint_softmax([(s_past, mask_p), (_dot_nt(qg, k_new[:, kcol].astype(BF16)), mask_n)])
        o_cmp = _dot(p_n.astype(BF16), k_new[:, vcol].astype(BF16))
        for n in range(bpp):
            o_cmp = o_cmp + _dot(p_p[:, n * hp:(n + 1) * hp].astype(BF16), vplanes[n])
        ocmp_ref[:, g * NSA_REP * HEAD_DIM:(g + 1) * NSA_REP * HEAD_DIM] = _heads_to_lanes(o_cmp, ts)

        pp = fold_heads(p_p)
        pn = fold_heads(p_n)
        imp_new = jnp.zeros((ts, LANES), F32)
        for c in range(n_new_blk):
            imp_new = imp_new + jnp.where(lane == c // 2, pn[:, c:c + 1], 0.0)
        imp = jnp.concatenate([pp[:, 0:hp] + pp[:, hp:2 * hp], pp[:, 2 * hp:3 * hp] + pp[:, 3 * hp:4 * hp],
                               imp_new], axis=1)
        cur = t8 // SEL_BLOCK
        forced = (j == 0) | (j == cur) | (j == cur - 1)
        valid = j * SEL_BLOCK <= t8
        score = jnp.where(forced, -NEG, jnp.where(valid, imp, NEG))
        alive = j < nsb
        chosen = jnp.zeros((ts, LANES), jnp.int32)
        for n in range(min(N_SELECT, nsb)):
            best = jnp.max(jnp.where(alive, score, 2.0 * NEG), axis=-1, keepdims=True)
            pick = jnp.min(jnp.where(alive & (score == best), j, nsb), axis=-1, keepdims=True)
            alive = alive & (j != pick)
            chosen = jnp.where(lane == n, pick, chosen)
        idx_ref[g] = chosen

        kt_old = w_old[g * HEAD_DIM:(g + 1) * HEAD_DIM, :].astype(BF16)
        vt_old = w_old[(NSA_KV_HEADS + g) * HEAD_DIM:(NSA_KV_HEADS + g + 1) * HEAD_DIM, :].astype(BF16)
        pw_old, pw_new = joint_softmax([(_dot(qg, kt_old), wmask_old),
                                        (_dot_nt(qg, w_new[:, kcol].astype(BF16)), wmask_new)])
        o_win = _dot_nt(pw_old.astype(BF16), vt_old) + _dot(pw_new.astype(BF16), w_new[:, vcol].astype(BF16))
        owin_ref[:, g * NSA_REP * HEAD_DIM:(g + 1) * NSA_REP * HEAD_DIM] = _heads_to_lanes(o_win, ts)


def _nsa_sample_a(q, kc_past, kc_new, win_old, win_new, *, b, past, n_new_blk, tv):
    hp = kc_past.shape[0] // b
    blk3 = lambda a: pl.BlockSpec((None,) + a.shape[1:], lambda bb: (bb, 0, 0))
    return pl.pallas_call(
        functools.partial(_nsa_sample_a_kernel, past=past, n_new_blk=n_new_blk, tv=tv),
        out_shape=(jax.ShapeDtypeStruct((b, TS, 512), F32), jax.ShapeDtypeStruct((b, TS, 512), F32),
                   jax.ShapeDtypeStruct((b, NSA_KV_HEADS, TS, LANES), jnp.int32),
                   jax.ShapeDtypeStruct(win_old.shape, F32)),
        grid=(b,),
        in_specs=[blk3(q), pl.BlockSpec((hp, kc_past.shape[1]), lambda bb: (bb, 0)), blk3(kc_new), blk3(win_old),
                  blk3(win_new)],
        out_specs=(pl.BlockSpec((None, TS, 512), lambda bb: (bb, 0, 0)),
                   pl.BlockSpec((None, TS, 512), lambda bb: (bb, 0, 0)),
                   pl.BlockSpec((None, NSA_KV_HEADS, TS, LANES), lambda bb: (bb, 0, 0, 0)),
                   blk3(win_old)),
        compiler_params=_cparams(("parallel",)),
        name="nsa_sample_a",
    )(q, kc_past, kc_new, win_old, win_new)


def _nsa_sel_kernel(idx_ref, pt_ref, pool_ref, new_ref, q_ref, ocmp_ref, owin_ref, gate_ref, o_ref,
                    buf_ref, sem_ref, *, past, base, n_pages, tv, nsel):
    bb = pl.program_id(0)
    nb = pl.num_programs(0)
    ts = q_ref.shape[0]
    page = pool_ref.shape[2]
    blocks_per_page = page // SEL_BLOCK
    nsb_past = past // SEL_BLOCK
    kv_rows = NSA_KV_HEADS * HEAD_DIM

    def block_id(b_, g, t, n):
        return idx_ref[((b_ * NSA_KV_HEADS + g) * ts + t) * nsel + n]

    def copies(src_slab, slot, g, t, n):
        lanes = pl.ds(pl.multiple_of(n * page, page), page)
        return [pltpu.make_async_copy(src_slab.at[pl.ds(kv * kv_rows + g * HEAD_DIM, HEAD_DIM), :],
                                      buf_ref.at[slot, kv, g * tv + t, :, lanes], sem_ref.at[slot])
                for kv in range(2)]

    def fetch(b_, slot):
        for g in range(NSA_KV_HEADS):
            for t in range(tv):
                def body(n, c):
                    bid = block_id(b_, g, t, n)

                    @pl.when(bid < nsb_past)
                    def _():
                        src = pool_ref.at[base + pt_ref[b_ * n_pages + bid // blocks_per_page]]
                        for cp in copies(src, slot, g, t, n):
                            cp.start()

                    @pl.when(bid >= nsb_past)
                    def _():
                        for cp in copies(new_ref.at[b_], slot, g, t, n):
                            cp.start()
                    return c
                lax.fori_loop(0, nsel, body, 0)

    @pl.when(bb == 0)
    def _():
        fetch(0, 0)

    @pl.when(bb + 1 < nb)
    def _():
        fetch(bb + 1, lax.rem(bb + 1, 2))

    slot = lax.rem(bb, 2)

    def wait_body(n, c):
        for cp in copies(pool_ref.at[0], slot, 0, 0, 0):
            cp.wait()
        return c
    lax.fori_loop(0, NSA_KV_HEADS * tv * nsel, wait_body, 0)

    q = q_ref[...].astype(F32)
    gates = gate_ref[...]
    o_cmp = ocmp_ref[...]
    o_win = owin_ref[...]
    m_rows = NSA_REP * ts
    row = lax.broadcasted_iota(jnp.int32, (m_rows, 1), 0)
    trow = lax.rem(row, ts)
    tpos = past + trow
    nk = nsel * page
    lane = lax.broadcasted_iota(jnp.int32, (1, nk), 1)
    in_page = lax.rem(lane, page)
    outs = []
    for g in range(NSA_KV_HEADS):
        qg = _stack_heads(q, g).astype(BF16)
        o_sel = jnp.zeros((m_rows, HEAD_DIM), F32)
        for t in range(tv):
            kt = buf_ref[slot, 0, g * tv + t].astype(BF16)
            vt = buf_ref[slot, 1, g * tv + t].astype(BF16)
            first_pos = jnp.zeros((1, nk), jnp.int32)
            half = jnp.zeros((1, nk), jnp.int32)
            for n in range(nsel):
                bid = block_id(bb, g, t, n)
                here = lane // page == n
                first_pos = jnp.where(here, (bid // blocks_per_page) * page, first_pos)
                half = jnp.where(here, lax.rem(bid, blocks_per_page), half)
            kpos = first_pos + in_page
            mask = (in_page // SEL_BLOCK == half) & (kpos <= tpos)
            s = jnp.where(mask, _dot(qg, kt), NEG)
            e = jnp.where(mask, jnp.exp(s - jnp.max(s, axis=-1, keepdims=True)), 0.0)
            p = e / jnp.sum(e, axis=-1, keepdims=True)
            o_sel = o_sel + jnp.where(trow == t, _dot_nt(p.astype(BF16), vt), 0.0)
        for r in range(NSA_REP):
            h = g * NSA_REP + r
            hs = slice(h * HEAD_DIM, (h + 1) * HEAD_DIM)
            outs.append(o_cmp[:, hs] * gates[:, 3 * h:3 * h + 1] + o_sel[r * ts:(r + 1) * ts] * gates[:, 3 * h + 1:3 * h + 2]
                        + o_win[:, hs] * gates[:, 3 * h + 2:3 * h + 3])
    o_ref[...] = jnp.concatenate(outs, axis=1)


def _nsa_sample_sel(idx_flat, page_table_flat, pool_slabs, new_slabs, q, o_cmp, o_win, gates, *, b, past, base, tv):
    nsel = idx_flat.shape[0] // (b * NSA_KV_HEADS * TS)
    page = pool_slabs.shape[2]
    blk3 = lambda a: pl.BlockSpec((None,) + a.shape[1:], lambda bb, i_, p_: (bb, 0, 0))
    return pl.pallas_call(
        functools.partial(_nsa_sel_kernel, past=past, base=base, n_pages=page_table_flat.shape[0] // b,
                          tv=tv, nsel=nsel),
        out_shape=jax.ShapeDtypeStruct((b, TS, 512), F32),
        grid_spec=pltpu.PrefetchScalarGridSpec(
            num_scalar_prefetch=2,
            grid=(b,),
            in_specs=[pl.BlockSpec(memory_space=pl.ANY), pl.BlockSpec(memory_space=pl.ANY),
                      blk3(q), blk3(o_cmp), blk3(o_win), blk3(gates)],
            out_specs=pl.BlockSpec((None, TS, 512), lambda bb, i_, p_: (bb, 0, 0)),
            scratch_shapes=[pltpu.VMEM((2, 2, NSA_KV_HEADS * tv, HEAD_DIM, nsel * page), F32),
                            pltpu.SemaphoreType.DMA((2,))]),
        compiler_params=_cparams(("arbitrary",)),
        name="nsa_sample_sel",
    )(idx_flat, page_table_flat, pool_slabs, new_slabs, q, o_cmp, o_win, gates)


def _layer_weights(l, norm_mix, w_in, gla_wa2, gla_ba, gla_norm, cmp_pe, cmp_w1, cmp_w2, w_out, norm_mem,
                   w_mq, w_mkv, w_mo, norm_ffn, w_ff1, w_ff2):
    pe_big, w1_big, w2_big = _pack_compress(cmp_pe[l], cmp_w1[l], cmp_w2[l])
    return dict(
        pages=_pack_compress_pages(cmp_pe[l], cmp_w1[l], cmp_w2[l]), norm_mix=norm_mix[l][None, :], w_in=_pack_w_in(w_in[l]), wa2=_pack_wa2(gla_wa2[l]),
        ba=gla_ba[l][None, :], gnorm=gla_norm[l][None, :], pe_big=pe_big, w1_big=w1_big, w2_big=w2_big,
        w_out=w_out[l].astype(BF16), norm_mem=norm_mem[l][None, :], w_mq=w_mq[l].astype(BF16),
        w_mkv=w_mkv[l].astype(BF16), w_mo=w_mo[l].astype(BF16), norm_ffn=norm_ffn[l][None, :],
        w_ff1=w_ff1[l].astype(BF16), w_ff2=w_ff2[l].astype(BF16))


def _even_odd(kcv, b):
    nb = kcv.shape[0] // b
    return kcv.reshape(b, nb // 2, 2, -1).transpose(0, 2, 1, 3).reshape(b * nb, -1)


def _prompt_layer(x, mem, p, tabs, gfinal, *, b, t, final_norm):
    q, cmp_rows, slc_rows, win_rows, gates, gla, ret = _project(
        x, p['norm_mix'], p['w_in'], p['wa2'], p['ba'], tabs, tm=256)
    kcv = _compress(cmp_rows.reshape(-1, CMP_BLOCK * KV_W), p['pe_big'], p['w1_big'], p['w2_big'], tr=128)
    o_nsa = _nsa_prompt(q, _even_odd(kcv, b), slc_rows, win_rows, gates, b=b, t=t, tq=128, tk=256)
    sg0 = jnp.zeros((b, GLA_HEADS, GLA_DK, GLA_DV), F32)
    sr0 = jnp.zeros((b, RET_HEADS, RET_DIM, RET_DIM), F32)
    o_rec, s_gla, s_ret = _recurrent(gla, ret, p['gnorm'], sg0, sr0, b=b, t=t, tt=128)
    x = _outproj(x, o_nsa, o_rec, p['w_out'], tm=512)
    mem_kv = _matmul(mem, p['w_mkv'], tm=256)
    x = _cross(x, p['norm_mem'], p['w_mq'], p['w_mo'], mem_kv, b=b, t=t, tq=256)
    x = _ffn(x, p['norm_ffn'], p['w_ff1'], p['w_ff2'], gfinal, tm=512, tf=1024, final_norm=final_norm)
    wlen = min(WINDOW, t)
    win_tail = win_rows.reshape(b, t, KV_W)[:, t - wlen:]
    return x, (cmp_rows, slc_rows, win_tail, s_gla, s_ret, mem_kv)


def _sample_layer(x, p, tabs, gfinal, cmp_slabs, slc_slabs, win_slabs, sg0, sr0, mem_kv, pt_flat,
                  *, b, past, tv, base, final_norm):
    n = b * TS
    q, cmp_rows, slc_rows, win_rows, gates, gla, ret = _project(
        x, p['norm_mix'], p['w_in'], p['wa2'], p['ba'], tabs, tm=n, valid_period=(TS, tv))
    rows3 = lambda a: a.reshape(b, TS, a.shape[-1])
    pad_blk = lambda a: jnp.pad(rows3(a), ((0, 0), (0, SEL_BLOCK - TS), (0, 0)))
    n_new_blk = SEL_BLOCK // CMP_BLOCK
    pg = min(64, past // PAGE_SIZE)
    kc_past = _compress_pages(pt_flat, cmp_slabs, *p['pages'], base=base, pg=pg)
    kc_new = _compress(pad_blk(cmp_rows).reshape(b * n_new_blk, CMP_BLOCK * KV_W), p['pe_big'], p['w1_big'],
                       p['w2_big'], tr=b * n_new_blk)
    kc_new = jnp.pad(kc_new.reshape(b, n_new_blk, KV_W), ((0, 0), (0, TS - n_new_blk), (0, 0)))
    q3 = rows3(q)
    o_cmp, o_win, idx, new_win = _nsa_sample_a(q3, kc_past, kc_new, win_slabs, rows3(win_rows), b=b, past=past,
                                               n_new_blk=n_new_blk, tv=tv)
    nsel = min(N_SELECT, past // SEL_BLOCK + 1)
    slc_new = jnp.pad(jnp.swapaxes(rows3(slc_rows), 1, 2), ((0, 0), (0, 0), (0, cmp_slabs.shape[2] - TS)))
    o_nsa = _nsa_sample_sel(idx[..., :nsel].reshape(-1), pt_flat, slc_slabs, slc_new, q3, o_cmp, o_win,
                            rows3(gates), b=b, past=past, base=base, tv=tv)
    o_rec, s_gla, s_ret = _recurrent(gla, ret, p['gnorm'], sg0, sr0, b=b, t=TS, tt=TS, n_valid=tv)
    x = _outproj(x, o_nsa.reshape(n, -1), o_rec, p['w_out'], tm=n)
    x = _cross(x, p['norm_mem'], p['w_mq'], p['w_mo'], mem_kv, b=b, t=TS, tq=TS)
    x = _ffn(x, p['norm_ffn'], p['w_ff1'], p['w_ff2'], gfinal, tm=n, tf=1024, final_norm=final_norm)
    return x, (rows3(cmp_rows)[:, :tv], rows3(slc_rows)[:, :tv], new_win, s_gla, s_ret)


def kernel(x_prompt, x_sample, mem_prompt, cache_cmp_kv, cache_slc_kv, cache_win_kv, state_gla, state_ret,
           cache_mem_kv, page_table, norm_mix, w_in, gla_wa2, gla_ba, gla_norm, cmp_pe, cmp_w1, cmp_w2, w_out,
           norm_mem, w_mq, w_mkv, w_mo, norm_ffn, w_ff1, w_ff2, norm_final):
    depth = w_in.shape[0]
    b, t, d = x_prompt.shape
    g, hd = NSA_KV_HEADS, HEAD_DIM
    gfinal = norm_final[None, :]
    tabs_p = _all_rope_tables(jnp.arange(t))
    hp = x_prompt.reshape(b * t, d)
    mem = mem_prompt.reshape(-1, d)
    sb, tv = x_sample.shape[:2]
    assert tv <= TS
    past = page_table.shape[1] * PAGE_SIZE
    tabs_s = _all_rope_tables(past + jnp.arange(sb * TS) % TS)
    hs = jnp.pad(x_sample, ((0, 0), (0, TS - tv), (0, 0))).reshape(sb * TS, d)
    pt_flat = page_table.reshape(-1)
    n_pool = cache_cmp_kv.shape[1]
    cmp_slabs = _pool_slabs(cache_cmp_kv)
    slc_slabs = _pool_slabs(cache_slc_kv)
    win_slabs = jnp.transpose(cache_win_kv, (0, 1, 3, 4, 5, 2)).reshape(depth, sb, KV_W, -1)
    new_p = [[] for _ in range(6)]
    new_s = [[] for _ in range(5)]
    for l in range(depth):
        p = _layer_weights(l, norm_mix, w_in, gla_wa2, gla_ba, gla_norm, cmp_pe, cmp_w1, cmp_w2, w_out, norm_mem,
                           w_mq, w_mkv, w_mo, norm_ffn, w_ff1, w_ff2)
        last = l == depth - 1
        hp, st = _prompt_layer(hp, mem, p, tabs_p, gfinal, b=b, t=t, final_norm=last)
        for lst, a in zip(new_p, st):
            lst.append(a)
        hs, st = _sample_layer(hs, p, tabs_s, gfinal, cmp_slabs, slc_slabs, win_slabs[l], state_gla[l], state_ret[l],
                               cache_mem_kv[l].reshape(-1, 2 * D_MODEL), pt_flat, b=sb, past=past, tv=tv,
                               base=l * n_pool, final_norm=last)
        for lst, a in zip(new_s, st):
            lst.append(a)
    cmp_p, slc_p, win_p, gla_p, ret_p, mem_p = [jnp.stack(a) for a in new_p]
    cmp_s, slc_s, win_s, gla_s, ret_s = [jnp.stack(a) for a in new_s]
    n_mem = mem_prompt.shape[1]
    kv6 = lambda a: a.reshape(a.shape[:3] + (2, g, hd))
    win_s = jnp.transpose(win_s.reshape(depth, sb, 2, g, hd, -1), (0, 1, 5, 2, 3, 4))
    return (hp.reshape(b, t, d), hs.reshape(sb, TS, d)[:, :tv],
            kv6(cmp_p.reshape(depth, b, t, KV_W)), kv6(slc_p.reshape(depth, b, t, KV_W)), kv6(win_p), gla_p, ret_p,
            mem_p.reshape(depth, b, n_mem, 2, MEM_HEADS, MEM_HD),
            kv6(cmp_s), kv6(slc_s), win_s, gla_s, ret_s)
```

```python
import functools
import math

import numpy as np
import jax
import jax.numpy as jnp
from jax import lax
from jax.experimental import pallas as pl
from jax.experimental.pallas import tpu as pltpu

F32 = jnp.float32
BF16 = jnp.bfloat16

D_MODEL = 1024
PAGE_SIZE = 128
HEAD_DIM = 64
NSA_HEADS = 8
NSA_KV_HEADS = 2
NSA_REP = NSA_HEADS // NSA_KV_HEADS
CMP_BLOCK = 32
SEL_BLOCK = 64
N_SELECT = 16
WINDOW = 512
ROT_DIM = HEAD_DIM // 4
ROPE_THETA = 500000.0
GLA_HEADS = 4
GLA_DK = 32
GLA_DV = 64
GLA_RANK = 16
GLA_TAU = 16.0
RET_HEADS = 4
RET_DIM = 64
RET_THETA = 10000.0
MEM_HEADS = 4
MEM_HD = D_MODEL // MEM_HEADS
D_FF = 4 * D_MODEL
EPS = 1e-6
IN_SIZES = (NSA_HEADS * HEAD_DIM, 6 * NSA_KV_HEADS * HEAD_DIM, 3 * NSA_HEADS,
            GLA_HEADS * GLA_DK, GLA_HEADS * GLA_DK, GLA_HEADS * GLA_DV, GLA_RANK, GLA_HEADS * GLA_DV,
            RET_HEADS * RET_DIM, RET_HEADS * RET_DIM, RET_HEADS * RET_DIM, RET_HEADS * RET_DIM)

LANES = 128
VMEM_LIMIT = 56 << 20
NEG = -1e30
M_INIT = -1e29
KV_W = 2 * NSA_KV_HEADS * HEAD_DIM

C_Q = 0
C_KV = 512
C_GLA = 1280
C_RET = 2048
C_TAIL = 3072
W_IN_COLS = 3200
GLA_W = 896
RET_W = 1024


def _cparams(sem, vmem=VMEM_LIMIT):
    return pltpu.CompilerParams(dimension_semantics=sem, vmem_limit_bytes=vmem)


def _dot(a, b):
    return jnp.dot(a, b, preferred_element_type=F32)


def _dot_nt(a, b):
    return lax.dot_general(a, b, (((1,), (1,)), ((), ())), preferred_element_type=F32)


def _dot_tn(a, b):
    return lax.dot_general(a, b, (((0,), (0,)), ((), ())), preferred_element_type=F32)


def _split3(x):
    hi = x.astype(BF16)
    r = x - hi.astype(F32)
    mid = r.astype(BF16)
    lo = (r - mid.astype(F32)).astype(BF16)
    return hi, mid, lo


def _rms(x, eps=EPS):
    return x * lax.rsqrt(jnp.mean(x * x, axis=-1, keepdims=True) + eps)


def _silu(x):
    return x * jax.nn.sigmoid(x)


def _rope128(v, c, sa, sb, half):
    return v * c + pltpu.roll(v, LANES - half, 1) * sa + pltpu.roll(v, half, 1) * sb


def _proj_kernel(x_ref, gain_ref, w_ref, wa2_ref, ba_ref, nc_ref, nsa_ref, nsb_ref, rc_ref, rsa_ref, rsb_ref,
                 q_ref, cmp_ref, slc_ref, win_ref, gate_ref, gla_ref, ret_ref, *, valid_period):
    x = x_ref[...]
    xn = _rms(x) * gain_ref[...]
    if valid_period is not None:
        period, n_valid = valid_period
        row = lax.broadcasted_iota(jnp.int32, (x.shape[0], 1), 0)
        xn = jnp.where(lax.rem(row, period) < n_valid, xn, 0.0)
    xb = xn.astype(BF16)

    def mm(a, b):
        return _dot(xb, w_ref[:, a:b])

    nc, nsa, nsb = nc_ref[...], nsa_ref[...], nsb_ref[...]
    rc, rsa, rsb = rc_ref[...], rsa_ref[...], rsb_ref[...]
    half_n = ROT_DIM // 2
    half_r = RET_DIM // 2

    q = mm(C_Q, C_KV)
    for c in range(4):
        sl = slice(c * LANES, (c + 1) * LANES)
        q_ref[:, sl] = (_rope128(q[:, sl], nc, nsa, nsb, half_n) * (HEAD_DIM ** -0.5)).astype(q_ref.dtype)

    kv = mm(C_KV, C_GLA)
    for br, ref in enumerate((cmp_ref, slc_ref, win_ref)):
        ref[:, 0:LANES] = _rope128(kv[:, br * KV_W:br * KV_W + LANES], nc, nsa, nsb, half_n)
        ref[:, LANES:KV_W] = kv[:, br * KV_W + LANES:(br + 1) * KV_W]

    tail = mm(C_TAIL, W_IN_COLS)
    gate_ref[...] = jax.nn.sigmoid(tail)
    z = jnp.dot(tail, wa2_ref[...], preferred_element_type=F32, precision=lax.Precision.HIGHEST) + ba_ref[...]
    log_a = (jnp.minimum(z, 0.0) - jnp.log1p(jnp.exp(-jnp.abs(z)))) * (1.0 / GLA_TAU)

    gl = mm(C_GLA, C_RET)
    gla_ref[:, 0:128] = gl[:, 0:128] * (GLA_DK ** -0.5)
    gla_ref[:, 128:256] = gl[:, 128:256]
    gla_ref[:, 256:384] = log_a
    gla_ref[:, 384:640] = gl[:, 256:512]
    gla_ref[:, 640:896] = _silu(gl[:, 512:768])

    rt = mm(C_RET, C_TAIL)
    for c in range(2):
        sl = slice(c * LANES, (c + 1) * LANES)
        ret_ref[:, sl] = _rope128(rt[:, sl], rc, rsa, rsb, half_r)
        sk = slice(256 + c * LANES, 256 + (c + 1) * LANES)
        ret_ref[:, sk] = _rope128(rt[:, sk], rc, rsa, rsb, half_r) * (RET_DIM ** -0.5)
    ret_ref[:, 512:768] = rt[:, 512:768]
    ret_ref[:, 768:1024] = _silu(rt[:, 768:1024])


def _project(x, gain, w, wa2, ba, tabs, *, tm, valid_period=None):
    n = x.shape[0]
    p = tabs[0].shape[0]
    nt = p // tm
    row = lambda w_: pl.BlockSpec((tm, w_), lambda i: (i, 0))
    full = lambda a: pl.BlockSpec(a.shape, lambda i: (0, 0))
    tab = pl.BlockSpec((tm, LANES), lambda i: (i % nt, 0))
    out_shape = (jax.ShapeDtypeStruct((n, 512), BF16),
                 jax.ShapeDtypeStruct((n, KV_W), F32), jax.ShapeDtypeStruct((n, KV_W), F32),
                 jax.ShapeDtypeStruct((n, KV_W), F32), jax.ShapeDtypeStruct((n, LANES), F32),
                 jax.ShapeDtypeStruct((n, GLA_W), F32), jax.ShapeDtypeStruct((n, RET_W), F32))
    return pl.pallas_call(
        functools.partial(_proj_kernel, valid_period=valid_period),
        out_shape=out_shape,
        grid=(n // tm,),
        in_specs=[row(D_MODEL), full(gain), full(w), full(wa2), full(ba)] + [tab] * 6,
        out_specs=(row(512), row(KV_W), row(KV_W), row(KV_W), row(LANES), row(GLA_W), row(RET_W)),
        compiler_params=_cparams(("parallel",)),
        name="proj",
    )(x, gain, w, wa2, ba, *tabs)


def _rope_tables(pos, inv_freq):
    nf = inv_freq.shape[0]
    ang = pos.astype(F32)[:, None] * inv_freq[None, :]
    cos, sin = jnp.cos(ang), jnp.sin(ang)
    ones = jnp.ones((pos.shape[0], HEAD_DIM - 2 * nf), F32)
    zeros = jnp.zeros((pos.shape[0], HEAD_DIM - nf), F32)
    c = jnp.concatenate([cos, cos, ones], axis=1)
    sa = jnp.concatenate([-sin, zeros], axis=1)
    sb = jnp.concatenate([zeros[:, :nf], sin, zeros[:, :HEAD_DIM - 2 * nf]], axis=1)
    return tuple(jnp.tile(t, (1, LANES // HEAD_DIM)) for t in (c, sa, sb))


def _all_rope_tables(pos):
    nsa_f = ROPE_THETA ** (-jnp.arange(0, ROT_DIM, 2, dtype=F32) / ROT_DIM)
    ret_f = RET_THETA ** (-jnp.linspace(0.0, 1.0, RET_DIM // 2, dtype=F32))
    return _rope_tables(pos, nsa_f) + _rope_tables(pos, ret_f)


def _pack_w_in(w):
    offs = np.cumsum((0,) + IN_SIZES)
    nq, nkv, ngate, gq, gk, gv, ga, gr, rq, rk, rv, rg = [w[:, offs[i]:offs[i + 1]] for i in range(12)]
    pad = jnp.zeros((w.shape[0], W_IN_COLS - C_TAIL - ngate.shape[1] - ga.shape[1]), w.dtype)
    return jnp.concatenate([nq, nkv, gq, gk, gv, gr, rq, rk, rv, rg, ngate, ga, pad], axis=1).astype(BF16)


def _pack_wa2(wa2):
    top = jnp.zeros((3 * NSA_HEADS, wa2.shape[1]), wa2.dtype)
    bot = jnp.zeros((LANES - 3 * NSA_HEADS - GLA_RANK, wa2.shape[1]), wa2.dtype)
    return jnp.concatenate([top, wa2, bot], axis=0)


def _gelu_tanh(x):
    return 0.5 * x * (1.0 + jnp.tanh(0.7978845608028654 * (x + 0.044715 * (x * x * x))))


def _compress_kernel(x_ref, pe_ref, w1_ref, w2_ref, o_ref):
    xb = (x_ref[...] + pe_ref[...]).astype(BF16)
    h = _gelu_tanh(_dot(xb, w1_ref[...]))
    o_ref[...] = _dot(h.astype(BF16), w2_ref[...])


def _compress(rows_flat, pe_big, w1_big, w2_big, *, tr):
    r, k = rows_flat.shape
    tr = min(tr, r)
    full = lambda a: pl.BlockSpec(a.shape, lambda i: (0, 0))
    return pl.pallas_call(
        _compress_kernel,
        out_shape=jax.ShapeDtypeStruct((r, KV_W), F32),
        grid=(r // tr,),
        in_specs=[pl.BlockSpec((tr, k), lambda i: (i, 0)), full(pe_big), full(w1_big), full(w2_big)],
        out_specs=pl.BlockSpec((tr, KV_W), lambda i: (i, 0)),
        compiler_params=_cparams(("parallel",)),
        name="compress",
    )(rows_flat, pe_big, w1_big, w2_big)


def _pack_compress(pe, w1, w2):
    g, d = NSA_KV_HEADS, HEAD_DIM
    hid = w1.shape[-1]
    pe_big = jnp.broadcast_to(pe.transpose(1, 0, 2)[:, :, None, :], (CMP_BLOCK, 2, g, d)).reshape(1, -1)
    eye = jnp.eye(2 * g, dtype=w1.dtype).reshape(2, g, 2, g)
    w1r = w1.reshape(2, CMP_BLOCK, d, hid)
    w1_big = jnp.einsum('kjdc,kgKG->jkgdKGc', w1r, eye).reshape(CMP_BLOCK * KV_W, 2 * g * hid)
    w2_big = jnp.einsum('kce,kgKG->kgcKGe', w2, eye).reshape(2 * g * hid, KV_W)
    return pe_big, w1_big.astype(BF16), w2_big.astype(BF16)


def _stack_heads(q, g):
    return jnp.concatenate([q[:, (g * NSA_REP + r) * HEAD_DIM:(g * NSA_REP + r + 1) * HEAD_DIM]
                            for r in range(NSA_REP)], axis=0)


def _flash_init(n_chains, nq):
    return tuple((jnp.full((1, nq), M_INIT, F32), jnp.zeros((1, nq), F32), jnp.zeros((HEAD_DIM, nq), F32))
                 for _ in range(n_chains))


def _flash_t(chains, tpos, lo, hi, tk, carry):
    def body(kt, carry):
        k0 = pl.multiple_of(kt * tk, tk)
        kpos = k0 + lax.broadcasted_iota(jnp.int32, (tk, 1), 0)
        out = []
        for (kv_ref, kcol, vcol, q_t, mask_fn), (m_old, l_old, acc) in zip(chains, carry):
            kb = kv_ref[pl.ds(k0, tk), kcol].astype(BF16)
            vb = kv_ref[pl.ds(k0, tk), vcol].astype(BF16)
            s_t = mask_fn(_dot(kb, q_t), k0, kpos, tpos)
            m_new = jnp.maximum(m_old, jnp.max(s_t, axis=0, keepdims=True))
            alpha = jnp.exp(m_old - m_new)
            p = jnp.exp(s_t - m_new)
            l_new = alpha * l_old + jnp.sum(p, axis=0, keepdims=True)
            out.append((m_new, l_new, alpha * acc + _dot_tn(vb, p.astype(BF16))))
        return tuple(out)

    return lax.fori_loop(lo, hi, body, carry)


def _nsa_prompt_kernel(q_ref, kcv_ref, slc_ref, win_ref, gate_ref, exp_ref, o_ref, selx_ref, *, tq, tk):
    i = pl.program_id(1)
    mq = NSA_REP * tq
    nb = kcv_ref.shape[0]
    nsb = nb // 2
    q_t = q_ref[...].astype(F32).T
    gate_t = gate_ref[...].T
    tcol = i * tq + lax.rem(lax.broadcasted_iota(jnp.int32, (1, mq), 1), tq)
    tq_col = i * tq + lax.broadcasted_iota(jnp.int32, (1, tq), 1)
    blk_row = lax.broadcasted_iota(jnp.int32, (nb, 1), 0)
    blk = jnp.where(blk_row < nsb, 2 * blk_row, 2 * (blk_row - nsb) + 1)
    cmask = (blk + 1) * CMP_BLOCK - 1 <= tcol
    jrow = lax.broadcasted_iota(jnp.int32, (nsb, 1), 0)
    cur = tq_col // SEL_BLOCK
    forced = (jrow == 0) | (jrow == cur) | (jrow == cur - 1)
    valid = jrow * SEL_BLOCK <= tq_col
    n_kt = ((i + 1) * tq + tk - 1) // tk
    win_lo = jnp.maximum(i * tq - WINDOW + 1, 0) // tk

    def sel_mask(g, causal):
        def fn(s, k0, kpos, tpos):
            bias = selx_ref[g, pl.ds(k0, tk), :]
            s = s + jnp.concatenate([bias] * NSA_REP, axis=1)
            return jnp.where(kpos <= tpos, s, NEG) if causal else s
        return fn

    def win_mask(s, k0, kpos, tpos):
        return jnp.where(kpos <= tpos, jnp.where(kpos > tpos - WINDOW, s, NEG), NEG)

    kcols = [slice(g * HEAD_DIM, (g + 1) * HEAD_DIM) for g in range(NSA_KV_HEADS)]
    vcols = [slice((NSA_KV_HEADS + g) * HEAD_DIM, (NSA_KV_HEADS + g + 1) * HEAD_DIM) for g in range(NSA_KV_HEADS)]
    qgs, o_cmps = [], []
    for g in range(NSA_KV_HEADS):
        qg = jnp.concatenate([q_t[(g * NSA_REP + r) * HEAD_DIM:(g * NSA_REP + r + 1) * HEAD_DIM, :]
                              for r in range(NSA_REP)], axis=1).astype(BF16)
        qgs.append(qg)
        kcol, vcol = kcols[g], vcols[g]

        s_t = _dot(kcv_ref[:, kcol].astype(BF16), qg)
        sm = jnp.where(cmask, s_t, NEG)
        mx = jnp.max(sm, axis=0, keepdims=True)
        mx = jnp.where(mx > 0.5 * NEG, mx, 0.0)
        e = jnp.where(cmask, jnp.exp(sm - mx), 0.0)
        den = jnp.sum(e, axis=0, keepdims=True)
        p = e / jnp.where(den > 0.0, den, 1.0)
        o_cmp = _dot_tn(kcv_ref[:, vcol].astype(BF16), p.astype(BF16))
        ps = p[:, 0:tq]
        for r in range(1, NSA_REP):
            ps = ps + p[:, r * tq:(r + 1) * tq]
        imp = ps[0:nsb] + ps[nsb:nb]

        score = jnp.where(forced, -NEG, jnp.where(valid, imp, NEG))
        rank = jnp.zeros((nsb, tq), F32)
        for c in range(nsb):
            sc = score[c:c + 1, :]
            beats = (sc > score) | ((sc == score) & (jrow > c))
            rank = rank + jnp.where(beats, 1.0, 0.0)
        sel = jnp.where(rank < float(N_SELECT), 1.0, 0.0).astype(BF16)
        selx_ref[g] = (_dot(exp_ref[...], sel) - 1.0) * (-NEG)
        o_cmps.append(o_cmp)

    groups = range(NSA_KV_HEADS)
    carry = _flash_t([(slc_ref, kcols[g], vcols[g], qgs[g], sel_mask(g, False)) for g in groups], tcol,
                     0, win_lo, tk, _flash_init(NSA_KV_HEADS, mq))
    carry = _flash_t([(slc_ref, kcols[g], vcols[g], qgs[g], sel_mask(g, True)) for g in groups]
                     + [(win_ref, kcols[g], vcols[g], qgs[g], win_mask) for g in groups], tcol,
                     win_lo, n_kt, tk, carry + _flash_init(NSA_KV_HEADS, mq))
    o_sels = [acc / l_fin for _, l_fin, acc in carry[:NSA_KV_HEADS]]
    o_wins = [acc / l_fin for _, l_fin, acc in carry[NSA_KV_HEADS:]]

    for g in groups:
        outs = []
        for r in range(NSA_REP):
            h = g * NSA_REP + r
            cs = slice(r * tq, (r + 1) * tq)
            outs.append(o_cmps[g][:, cs] * gate_t[3 * h:3 * h + 1, :]
                        + o_sels[g][:, cs] * gate_t[3 * h + 1:3 * h + 2, :]
                        + o_wins[g][:, cs] * gate_t[3 * h + 2:3 * h + 3, :])
        o_ref[:, g * NSA_REP * HEAD_DIM:(g + 1) * NSA_REP * HEAD_DIM] = jnp.concatenate(outs, axis=0).T


def _nsa_prompt(q, kcv, slc, win, gates, *, b, t, tq, tk):
    n = b * t
    nq = t // tq
    nb = t // CMP_BLOCK
    nsb = t // SEL_BLOCK
    tk = min(tk, t)
    expand = (np.arange(t)[:, None] // SEL_BLOCK == np.arange(nsb)[None, :]).astype(np.float32)
    expand = jnp.asarray(expand, BF16)
    return pl.pallas_call(
        functools.partial(_nsa_prompt_kernel, tq=tq, tk=tk),
        out_shape=jax.ShapeDtypeStruct((n, NSA_HEADS * HEAD_DIM), F32),
        grid=(b, nq),
        in_specs=[pl.BlockSpec((tq, 512), lambda bb, i: (bb * nq + i, 0)),
                  pl.BlockSpec((nb, KV_W), lambda bb, i: (bb, 0)),
                  pl.BlockSpec((t, KV_W), lambda bb, i: (bb, 0)),
                  pl.BlockSpec((t, KV_W), lambda bb, i: (bb, 0)),
                  pl.BlockSpec((tq, LANES), lambda bb, i: (bb * nq + i, 0)),
                  pl.BlockSpec((t, nsb), lambda bb, i: (0, 0))],
        out_specs=pl.BlockSpec((tq, 512), lambda bb, i: (bb * nq + i, 0)),
        scratch_shapes=[pltpu.VMEM((NSA_KV_HEADS, t, tq), F32)],
        compiler_params=_cparams(("parallel", "arbitrary")),
        name="nsa_prompt",
    )(q, kcv, slc, win, gates, expand)


def _gla_level_matrix(tt):
    lv = int(math.log2(tt))
    m = np.zeros((lv, tt, tt), np.float32)
    t = np.arange(tt)
    for l in range(lv):
        half = 1 << l
        split = ((t >> (l + 1)) << (l + 1)) + half - 1
        u = np.arange(tt)[None, :]
        upper = t > split
        m[l] = np.where(upper[:, None], (u > split[:, None]) & (u <= t[:, None]),
                        (u > t[:, None]) & (u <= split[:, None]))
    return m.reshape(lv * tt, tt)


def _ret_log_decay():
    return [float(np.log(np.float32(1.0) - np.float32(2.0) ** np.float32(-5.0 - h))) for h in range(RET_HEADS)]


def _recur_kernel(gla_ref, ret_ref, gn_ref, lvl_ref, tril_ref, sg0_ref, sr0_ref,
                  o_ref, sg_out_ref, sr_out_ref, sg_ref, sr_ref, *, tt, n_valid):
    ti = pl.program_id(1)
    nt = pl.num_programs(1)
    levels = int(math.log2(tt))

    @pl.when(ti == 0)
    def _():
        sg_ref[...] = sg0_ref[...]
        sr_ref[...] = sr0_ref[...]

    rowi = lax.broadcasted_iota(jnp.int32, (tt, 1), 0)
    coli = lax.broadcasted_iota(jnp.int32, (1, tt), 1)

    gq = gla_ref[:, 0:128]
    gk = gla_ref[:, 128:256]
    la = gla_ref[:, 256:384]
    gv = gla_ref[:, 384:640]
    gr = gla_ref[:, 640:896]
    if n_valid < tt:
        la = jnp.where(rowi < n_valid, la, 0.0)
    parts = _split3(la)
    tril = tril_ref[...]
    lvl = lvl_ref[...]
    ones = jnp.ones((tt, GLA_DV), BF16)
    cum = sum(_dot(tril, pt) for pt in parts)
    dlv = sum(_dot(lvl, pt) for pt in parts)
    tot = sum(_dot_tn(pt, ones) for pt in parts)
    q_dec = gq * jnp.exp(cum)
    k_dec = gk * jnp.exp(cum[tt - 1:tt, :] - cum)
    gla_out = []
    for h in range(GLA_HEADS):
        sl = slice(h * GLA_DK, (h + 1) * GLA_DK)
        vs = slice(h * GLA_DV, (h + 1) * GLA_DV)
        qh, kh = gq[:, sl], gk[:, sl]
        vh = gv[:, vs].astype(BF16)
        attn = jnp.where(rowi == coli, _dot_nt(qh.astype(BF16), kh.astype(BF16)), 0.0)
        for l in range(levels):
            ed = jnp.exp(dlv[l * tt:(l + 1) * tt, sl])
            upper = ((rowi >> l) & 1) == 1
            qe = jnp.where(upper, qh * ed, 0.0).astype(BF16)
            ke = jnp.where(upper, 0.0, kh * ed).astype(BF16)
            same = (rowi >> (l + 1)) == (coli >> (l + 1))
            attn = attn + jnp.where(same, _dot_nt(qe, ke), 0.0)
        s_h = sg_ref[h]
        o = _dot(attn.astype(BF16), vh) + _dot(q_dec[:, sl].astype(BF16), s_h.astype(BF16))
        sg_ref[h] = jnp.exp(tot[sl, :]) * s_h + _dot_tn(k_dec[:, sl].astype(BF16), vh)
        gla_out.append(_rms(o) * gn_ref[...] * gr[:, vs])
    o_ref[:, 0:256] = jnp.concatenate(gla_out, axis=1)

    rel = (rowi - coli).astype(F32)
    pos1 = (rowi + 1).astype(F32)
    left = (n_valid - 1 - rowi).astype(F32)
    ret_out = []
    for h, lg in enumerate(_ret_log_decay()):
        sl = slice(h * RET_DIM, (h + 1) * RET_DIM)
        qh = ret_ref[:, sl]
        kh = ret_ref[:, 256 + h * RET_DIM:256 + (h + 1) * RET_DIM]
        vh = ret_ref[:, 512 + h * RET_DIM:512 + (h + 1) * RET_DIM].astype(BF16)
        gh = ret_ref[:, 768 + h * RET_DIM:768 + (h + 1) * RET_DIM]
        dmat = jnp.where(rel >= 0.0, jnp.exp(lg * jnp.maximum(rel, 0.0)), 0.0)
        attn = _dot_nt(qh.astype(BF16), kh.astype(BF16)) * dmat
        s_h = sr_ref[h]
        o = _dot(attn.astype(BF16), vh) + _dot(qh.astype(BF16), s_h.astype(BF16)) * jnp.exp(lg * pos1)
        sr_ref[h] = math.exp(lg * n_valid) * s_h + _dot_tn((kh * jnp.exp(lg * left)).astype(BF16), vh)
        ret_out.append(_rms(o) * gh)
    o_ref[:, 256:512] = jnp.concatenate(ret_out, axis=1)

    @pl.when(ti == nt - 1)
    def _():
        sg_out_ref[...] = sg_ref[...]
        sr_out_ref[...] = sr_ref[...]


def _recurrent(gla, ret, gnorm, sg0, sr0, *, b, t, tt, n_valid=None):
    n = b * t
    nt = t // tt
    n_valid = tt if n_valid is None else n_valid
    assert n_valid == tt or nt == 1
    lvl = jnp.asarray(_gla_level_matrix(tt), BF16)
    tril = jnp.asarray(np.tril(np.ones((tt, tt), np.float32)), BF16)
    full2 = lambda a: pl.BlockSpec(a.shape, lambda bb, i: (0, 0))
    st = lambda a: pl.BlockSpec((None,) + a.shape[1:], lambda bb, i: (bb, 0, 0, 0))
    return pl.pallas_call(
        functools.partial(_recur_kernel, tt=tt, n_valid=n_valid),
        out_shape=(jax.ShapeDtypeStruct((n, 512), F32), jax.ShapeDtypeStruct(sg0.shape, F32),
                   jax.ShapeDtypeStruct(sr0.shape, F32)),
        grid=(b, nt),
        in_specs=[pl.BlockSpec((tt, GLA_W), lambda bb, i: (bb * nt + i, 0)),
                  pl.BlockSpec((tt, RET_W), lambda bb, i: (bb * nt + i, 0)),
                  full2(gnorm), full2(lvl), full2(tril), st(sg0), st(sr0)],
        out_specs=(pl.BlockSpec((tt, 512), lambda bb, i: (bb * nt + i, 0)), st(sg0), st(sr0)),
        scratch_shapes=[pltpu.VMEM(sg0.shape[1:], F32), pltpu.VMEM(sr0.shape[1:], F32)],
        compiler_params=_cparams(("parallel", "arbitrary")),
        name="recurrent",
    )(gla, ret, gnorm, lvl, tril, sg0, sr0)


def _outproj_kernel(x_ref, a_ref, b_ref, w_ref, o_ref):
    ka = a_ref.shape[1]
    o_ref[...] = (x_ref[...] + _dot(a_ref[...].astype(BF16), w_ref[0:ka, :])
                  + _dot(b_ref[...].astype(BF16), w_ref[ka:, :]))


def _outproj(x, a, b_, w, *, tm):
    n = x.shape[0]
    row = lambda a_: pl.BlockSpec((tm, a_.shape[1]), lambda i: (i, 0))
    return pl.pallas_call(
        _outproj_kernel,
        out_shape=jax.ShapeDtypeStruct(x.shape, F32),
        grid=(n // tm,),
        in_specs=[row(x), row(a), row(b_), pl.BlockSpec(w.shape, lambda i: (0, 0))],
        out_specs=row(x),
        compiler_params=_cparams(("parallel",)),
        name="outproj",
    )(x, a, b_, w)


def _cross_kernel(x_ref, gain_ref, wq_ref, wo_ref, kv_ref, o_ref):
    x = x_ref[...]
    xb = (_rms(x) * gain_ref[...]).astype(BF16)
    q = _dot(xb, wq_ref[...])
    heads = []
    for h in range(MEM_HEADS):
        qh = q[:, h * MEM_HD:(h + 1) * MEM_HD].astype(BF16)
        kh = kv_ref[:, h * MEM_HD:(h + 1) * MEM_HD].astype(BF16)
        vh = kv_ref[:, (MEM_HEADS + h) * MEM_HD:(MEM_HEADS + h + 1) * MEM_HD].astype(BF16)
        s = _dot_nt(qh, kh) * (MEM_HD ** -0.5)
        e = jnp.exp(s - jnp.max(s, axis=-1, keepdims=True))
        p = e / jnp.sum(e, axis=-1, keepdims=True)
        heads.append(_dot(p.astype(BF16), vh))
    att = jnp.concatenate(heads, axis=1).astype(BF16)
    o_ref[...] = x + _dot(att, wo_ref[...])


def _cross(x, gain, wq, wo, memkv, *, b, t, tq):
    nq = t // tq
    n_mem = memkv.shape[0] // b
    full = lambda a: pl.BlockSpec(a.shape, lambda bb, i: (0, 0))
    row = pl.BlockSpec((tq, D_MODEL), lambda bb, i: (bb * nq + i, 0))
    return pl.pallas_call(
        _cross_kernel,
        out_shape=jax.ShapeDtypeStruct(x.shape, F32),
        grid=(b, nq),
        in_specs=[row, full(gain), full(wq), full(wo),
                  pl.BlockSpec((n_mem, memkv.shape[1]), lambda bb, i: (bb, 0))],
        out_specs=row,
        compiler_params=_cparams(("parallel", "arbitrary")),
        name="cross",
    )(x, gain, wq, wo, memkv)


def _ffn_kernel(x_ref, gain_ref, w1_ref, w2_ref, gf_ref, o_ref, xn_ref, acc_ref, *, final_norm):
    j = pl.program_id(1)

    @pl.when(j == 0)
    def _():
        xn_ref[...] = (_rms(x_ref[...]) * gain_ref[...]).astype(BF16)
        acc_ref[...] = x_ref[...]

    h = jnp.maximum(_dot(xn_ref[...], w1_ref[...]), 0.0)
    acc_ref[...] += _dot((h * h).astype(BF16), w2_ref[...])

    @pl.when(j == pl.num_programs(1) - 1)
    def _():
        y = acc_ref[...]
        if final_norm:
            y = _rms(y) * gf_ref[...]
        o_ref[...] = y


def _ffn(x, gain, w1, w2, gfinal, *, tm, tf, final_norm):
    n = x.shape[0]
    row = pl.BlockSpec((tm, D_MODEL), lambda i, j: (i, 0))
    vec = pl.BlockSpec((1, D_MODEL), lambda i, j: (0, 0))
    return pl.pallas_call(
        functools.partial(_ffn_kernel, final_norm=final_norm),
        out_shape=jax.ShapeDtypeStruct(x.shape, F32),
        grid=(n // tm, D_FF // tf),
        in_specs=[row, vec, pl.BlockSpec((D_MODEL, tf), lambda i, j: (0, j)),
                  pl.BlockSpec((tf, D_MODEL), lambda i, j: (j, 0)), vec],
        out_specs=row,
        scratch_shapes=[pltpu.VMEM((tm, D_MODEL), BF16), pltpu.VMEM((tm, D_MODEL), F32)],
        compiler_params=_cparams(("parallel", "arbitrary")),
        name="ffn",
    )(x, gain, w1, w2, gfinal)


def _matmul_kernel(x_ref, w_ref, o_ref):
    o_ref[...] = _dot(x_ref[...].astype(BF16), w_ref[...])


def _matmul(x, w, *, tm):
    n, k = x.shape
    return pl.pallas_call(
        _matmul_kernel,
        out_shape=jax.ShapeDtypeStruct((n, w.shape[1]), F32),
        grid=(n // tm,),
        in_specs=[pl.BlockSpec((tm, k), lambda i: (i, 0)), pl.BlockSpec(w.shape, lambda i: (0, 0))],
        out_specs=pl.BlockSpec((tm, w.shape[1]), lambda i: (i, 0)),
        compiler_params=_cparams(("parallel",)),
        name="memkv",
    )(x, w)


TS = 8


def _pool_slabs(pool):
    depth, n_pool = pool.shape[:2]
    return jnp.transpose(pool, (0, 1, 3, 4, 5, 2)).reshape(depth * n_pool, KV_W, pool.shape[2])


def _cmp_pages_kernel(pt_ref, pool_ref, wexp_hbm, pe_ref, w1_ref, w2_ref, o_ref, buf_ref, wbuf_ref, sem_ref,
                      *, pg, base):
    s = pl.program_id(0)
    ns = pl.num_programs(0)
    slab = pool_ref.shape[1]
    pitch = buf_ref.shape[1] // pg
    n_pairs = wbuf_ref.shape[1]
    blocks_per_page = PAGE_SIZE // CMP_BLOCK

    def page_copy(page, slot, j):
        return pltpu.make_async_copy(pool_ref.at[page], buf_ref.at[slot, pl.ds(j * pitch, slab), :], sem_ref.at[slot])

    def fetch(step, slot):
        def body(j, c):
            page_copy(base + pt_ref[step * pg + j], slot, j).start()
            return c
        lax.fori_loop(0, pg, body, 0)

    @pl.when(s == 0)
    def _():
        weights = pltpu.make_async_copy(wexp_hbm, wbuf_ref, sem_ref.at[2])
        weights.start()
        fetch(0, 0)
        weights.wait()

    @pl.when(s + 1 < ns)
    def _():
        fetch(s + 1, lax.rem(s + 1, 2))

    slot = lax.rem(s, 2)

    def wait_body(j, c):
        page_copy(0, slot, j).wait()
        return c
    lax.fori_loop(0, pg, wait_body, 0)

    for kv in range(2):
        pew = sum(_dot(part, w1_ref[kv]) for part in _split3(pe_ref[kv]))[0:1]
        pew = jnp.concatenate([pew] * blocks_per_page, axis=1)
        for g in range(NSA_KV_HEADS):
            kvg = kv * NSA_KV_HEADS + g

            def body(dd, acc):
                r0 = kvg * HEAD_DIM + 2 * dd
                x0 = buf_ref[slot, pl.ds(r0, pg, stride=pitch), :]
                x1 = buf_ref[slot, pl.ds(r0 + 1, pg, stride=pitch), :]
                x = jnp.concatenate([x0, x1], axis=1).astype(BF16)
                return acc + _dot(x, wbuf_ref[kv, dd])
            acc = lax.fori_loop(0, n_pairs, body, jnp.zeros((pg, w2_ref.shape[1]), F32), unroll=8)
            width = w2_ref.shape[2]
            o_ref[:, kvg * width:(kvg + 1) * width] = _dot(_gelu_tanh(acc + pew).astype(BF16), w2_ref[kv])


def _pack_compress_pages(pe, w1, w2):
    d = HEAD_DIM
    nblk = PAGE_SIZE // CMP_BLOCK
    hid = w1.shape[-1]
    eye = jnp.eye(nblk, dtype=w1.dtype)
    w1r = w1.reshape(2, CMP_BLOCK, d, hid)
    wexp = jnp.einsum('kjdc,nm->kdnjmc', w1r, eye).reshape(2, d // 2, 2 * PAGE_SIZE, nblk * hid)
    w2bd = jnp.einsum('kce,nm->kncme', w2, eye).reshape(2, nblk * hid, nblk * w2.shape[-1])
    pe_rows = jnp.pad(pe.reshape(2, 1, CMP_BLOCK * d), ((0, 0), (0, 7), (0, 0)))
    return wexp.astype(BF16), pe_rows, w1.astype(BF16), w2bd.astype(BF16)


def _compress_pages(page_table_flat, pool_slabs, wexp, pe_rows, w1b, w2bd, *, base, pg):
    n_pages = page_table_flat.shape[0]
    slab = pool_slabs.shape[1]
    width = w2bd.shape[2]
    full3 = lambda a: pl.BlockSpec(a.shape, lambda i, pt: (0, 0, 0))
    return pl.pallas_call(
        functools.partial(_cmp_pages_kernel, pg=pg, base=base),
        out_shape=jax.ShapeDtypeStruct((n_pages, 2 * NSA_KV_HEADS * width), F32),
        grid_spec=pltpu.PrefetchScalarGridSpec(
            num_scalar_prefetch=1,
            grid=(n_pages // pg,),
            in_specs=[pl.BlockSpec(memory_space=pl.ANY), pl.BlockSpec(memory_space=pl.ANY),
                      full3(pe_rows), full3(w1b), full3(w2bd)],
            out_specs=pl.BlockSpec((pg, 2 * NSA_KV_HEADS * width), lambda i, pt: (i, 0)),
            scratch_shapes=[pltpu.VMEM((2, pg * (slab + 8), pool_slabs.shape[2]), F32), pltpu.VMEM(wexp.shape, BF16),
                            pltpu.SemaphoreType.DMA((3,))]),
        compiler_params=_cparams(("arbitrary",)),
        name="compress_pages",
    )(page_table_flat, pool_slabs, wexp, pe_rows, w1b, w2bd)


def _heads_to_lanes(o, ts):
    return jnp.concatenate([o[r * ts:(r + 1) * ts] for r in range(NSA_REP)], axis=1)


def _nsa_sample_a_kernel(q_ref, kcp_ref, kcn_ref, wc_ref, wn_ref, ocmp_ref, owin_ref, idx_ref, wout_ref,
                         *, past, n_new_blk, tv):
    ts = q_ref.shape[0]
    q = q_ref[...].astype(F32)
    hp = kcp_ref.shape[0]
    bpp = PAGE_SIZE // CMP_BLOCK
    plane_w = bpp * HEAD_DIM
    nbp = hp * bpp
    nsb_past = nbp * CMP_BLOCK // SEL_BLOCK
    nsb = nsb_past + (n_new_blk + 1) // 2
    wb = wc_ref.shape[1]
    m_rows = NSA_REP * ts
    row = lax.broadcasted_iota(jnp.int32, (m_rows, 1), 0)
    tpos = past + lax.rem(row, ts)
    t8 = past + lax.broadcasted_iota(jnp.int32, (ts, 1), 0)
    k_new = kcn_ref[...]
    w_old = wc_ref[...]
    w_new = wn_ref[...]
    nn = k_new.shape[0]
    pagei = lax.broadcasted_iota(jnp.int32, (1, hp), 1)
    cn = lax.broadcasted_iota(jnp.int32, (1, nn), 1)
    mask_p = jnp.concatenate([(bpp * pagei + n + 1) * CMP_BLOCK - 1 <= tpos for n in range(bpp)], axis=1)
    mask_n = ((nbp + cn + 1) * CMP_BLOCK - 1 <= tpos) & (cn < n_new_blk)
    kp_old = past - wb + lax.broadcasted_iota(jnp.int32, (1, wb), 1)
    kp_new = past + lax.broadcasted_iota(jnp.int32, (1, w_new.shape[0]), 1)
    wmask_old = (kp_old <= tpos) & (kp_old > tpos - WINDOW) & (kp_old >= 0)
    wmask_new = (kp_new <= tpos) & (kp_new > tpos - WINDOW)
    lane = lax.broadcasted_iota(jnp.int32, (1, LANES), 1)
    j = jnp.concatenate([2 * pagei, 2 * pagei + 1, nsb_past + lane], axis=1)
    width = 2 * hp + LANES

    new_t = jnp.concatenate([w_new, jnp.zeros((LANES - w_new.shape[0], w_new.shape[1]), F32)], axis=0).T
    wout_ref[...] = jnp.concatenate([w_old[:, tv:], new_t[:, 0:tv]], axis=1)

    def joint_softmax(parts):
        mx = None
        for s, mk in parts:
            cur = jnp.max(jnp.where(mk, s, NEG), axis=-1, keepdims=True)
            mx = cur if mx is None else jnp.maximum(mx, cur)
        mx = jnp.where(mx > 0.5 * NEG, mx, 0.0)
        es = [jnp.where(mk, jnp.exp(jnp.where(mk, s, NEG) - mx), 0.0) for s, mk in parts]
        den = sum(jnp.sum(e, axis=-1, keepdims=True) for e in es)
        inv = 1.0 / jnp.where(den > 0.0, den, 1.0)
        return [e * inv for e in es]

    def fold_heads(p):
        out = p[0:ts]
        for r in range(1, NSA_REP):
            out = out + p[r * ts:(r + 1) * ts]
        return out

    for g in range(NSA_KV_HEADS):
        qg = _stack_heads(q, g).astype(BF16)
        kcol = slice(g * HEAD_DIM, (g + 1) * HEAD_DIM)
        vcol = slice((NSA_KV_HEADS + g) * HEAD_DIM, (NSA_KV_HEADS + g + 1) * HEAD_DIM)

        kplanes = [kcp_ref[:, g * plane_w + n * HEAD_DIM:g * plane_w + (n + 1) * HEAD_DIM].astype(BF16)
                   for n in range(bpp)]
        vplanes = [kcp_ref[:, (NSA_KV_HEADS + g) * plane_w + n * HEAD_DIM:
                           (NSA_KV_HEADS + g) * plane_w + (n + 1) * HEAD_DIM].astype(BF16) for n in range(bpp)]
        s_past = jnp.concatenate([_dot_nt(qg, kp) for kp in kplanes], axis=1)
        p_p, p_n = joint_softmax([(s_past, mask_p), (_dot_nt(qg, k_new[:, kcol].astype(BF16)), mask_n)])
        o_cmp = _dot(p_n.astype(BF16), k_new[:, vcol].astype(BF16))
        for n in range(bpp):
            o_cmp = o_cmp + _dot(p_p[:, n * hp:(n + 1) * hp].astype(BF16), vplanes[n])
        ocmp_ref[:, g * NSA_REP * HEAD_DIM:(g + 1) * NSA_REP * HEAD_DIM] = _heads_to_lanes(o_cmp, ts)

        pp = fold_heads(p_p)
        pn = fold_heads(p_n)
        imp_new = jnp.zeros((ts, LANES), F32)
        for c in range(n_new_blk):
            imp_new = imp_new + jnp.where(lane == c // 2, pn[:, c:c + 1], 0.0)
        imp = jnp.concatenate([pp[:, 0:hp] + pp[:, hp:2 * hp], pp[:, 2 * hp:3 * hp] + pp[:, 3 * hp:4 * hp],
                               imp_new], axis=1)
        cur = t8 // SEL_BLOCK
        forced = (j == 0) | (j == cur) | (j == cur - 1)
        valid = j * SEL_BLOCK <= t8
        score = jnp.where(forced, -NEG, jnp.where(valid, imp, NEG))
        alive = j < nsb
        chosen = jnp.zeros((ts, LANES), jnp.int32)
        for n in range(min(N_SELECT, nsb)):
            best = jnp.max(jnp.where(alive, score, 2.0 * NEG), axis=-1, keepdims=True)
            pick = jnp.min(jnp.where(alive & (score == best), j, nsb), axis=-1, keepdims=True)
            alive = alive & (j != pick)
            chosen = jnp.where(lane == n, pick, chosen)
        idx_ref[g] = chosen

        kt_old = w_old[g * HEAD_DIM:(g + 1) * HEAD_DIM, :].astype(BF16)
        vt_old = w_old[(NSA_KV_HEADS + g) * HEAD_DIM:(NSA_KV_HEADS + g + 1) * HEAD_DIM, :].astype(BF16)
        pw_old, pw_new = joint_softmax([(_dot(qg, kt_old), wmask_old),
                                        (_dot_nt(qg, w_new[:, kcol].astype(BF16)), wmask_new)])
        o_win = _dot_nt(pw_old.astype(BF16), vt_old) + _dot(pw_new.astype(BF16), w_new[:, vcol].astype(BF16))
        owin_ref[:, g * NSA_REP * HEAD_DIM:(g + 1) * NSA_REP * HEAD_DIM] = _heads_to_lanes(o_win, ts)


def _nsa_sample_a(q, kc_past, kc_new, win_old, win_new, *, b, past, n_new_blk, tv):
    hp = kc_past.shape[0] // b
    blk3 = lambda a: pl.BlockSpec((None,) + a.shape[1:], lambda bb: (bb, 0, 0))
    return pl.pallas_call(
        functools.partial(_nsa_sample_a_kernel, past=past, n_new_blk=n_new_blk, tv=tv),
        out_shape=(jax.ShapeDtypeStruct((b, TS, 512), F32), jax.ShapeDtypeStruct((b, TS, 512), F32),
                   jax.ShapeDtypeStruct((b, NSA_KV_HEADS, TS, LANES), jnp.int32),
                   jax.ShapeDtypeStruct(win_old.shape, F32)),
        grid=(b,),
        in_specs=[blk3(q), pl.BlockSpec((hp, kc_past.shape[1]), lambda bb: (bb, 0)), blk3(kc_new), blk3(win_old),
                  blk3(win_new)],
        out_specs=(pl.BlockSpec((None, TS, 512), lambda bb: (bb, 0, 0)),
                   pl.BlockSpec((None, TS, 512), lambda bb: (bb, 0, 0)),
                   pl.BlockSpec((None, NSA_KV_HEADS, TS, LANES), lambda bb: (bb, 0, 0, 0)),
                   blk3(win_old)),
        compiler_params=_cparams(("parallel",)),
        name="nsa_sample_a",
    )(q, kc_past, kc_new, win_old, win_new)


def _nsa_sel_kernel(idx_ref, pt_ref, pool_ref, new_ref, q_ref, ocmp_ref, owin_ref, gate_ref, o_ref,
                    buf_ref, sem_ref, *, past, base, n_pages, tv, nsel):
    bb = pl.program_id(0)
    nb = pl.num_programs(0)
    ts = q_ref.shape[0]
    page = pool_ref.shape[2]
    blocks_per_page = page // SEL_BLOCK
    nsb_past = past // SEL_BLOCK
    kv_rows = NSA_KV_HEADS * HEAD_DIM

    def block_id(b_, g, t, n):
        return idx_ref[((b_ * NSA_KV_HEADS + g) * ts + t) * nsel + n]

    def copies(src_slab, slot, g, t, n):
        lanes = pl.ds(pl.multiple_of(n * page, page), page)
        return [pltpu.make_async_copy(src_slab.at[pl.ds(kv * kv_rows + g * HEAD_DIM, HEAD_DIM), :],
                                      buf_ref.at[slot, kv, g * tv + t, :, lanes], sem_ref.at[slot])
                for kv in range(2)]

    def fetch(b_, slot):
        for g in range(NSA_KV_HEADS):
            for t in range(tv):
                def body(n, c):
                    bid = block_id(b_, g, t, n)

                    @pl.when(bid < nsb_past)
                    def _():
                        src = pool_ref.at[base + pt_ref[b_ * n_pages + bid // blocks_per_page]]
                        for cp in copies(src, slot, g, t, n):
                            cp.start()

                    @pl.when(bid >= nsb_past)
                    def _():
                        for cp in copies(new_ref.at[b_], slot, g, t, n):
                            cp.start()
                    return c
                lax.fori_loop(0, nsel, body, 0)

    @pl.when(bb == 0)
    def _():
        fetch(0, 0)

    @pl.when(bb + 1 < nb)
    def _():
        fetch(bb + 1, lax.rem(bb + 1, 2))

    slot = lax.rem(bb, 2)

    def wait_body(n, c):
        for cp in copies(pool_ref.at[0], slot, 0, 0, 0):
            cp.wait()
        return c
    lax.fori_loop(0, NSA_KV_HEADS * tv * nsel, wait_body, 0)

    q = q_ref[...].astype(F32)
    gates = gate_ref[...]
    o_cmp = ocmp_ref[...]
    o_win = owin_ref[...]
    m_rows = NSA_REP * ts
    row = lax.broadcasted_iota(jnp.int32, (m_rows, 1), 0)
    trow = lax.rem(row, ts)
    tpos = past + trow
    nk = nsel * page
    lane = lax.broadcasted_iota(jnp.int32, (1, nk), 1)
    in_page = lax.rem(lane, page)
    outs = []
    for g in range(NSA_KV_HEADS):
        qg = _stack_heads(q, g).astype(BF16)
        o_sel = jnp.zeros((m_rows, HEAD_DIM), F32)
        for t in range(tv):
            kt = buf_ref[slot, 0, g * tv + t].astype(BF16)
            vt = buf_ref[slot, 1, g * tv + t].astype(BF16)
            first_pos = jnp.zeros((1, nk), jnp.int32)
            half = jnp.zeros((1, nk), jnp.int32)
            for n in range(nsel):
                bid = block_id(bb, g, t, n)
                here = lane // page == n
                first_pos = jnp.where(here, (bid // blocks_per_page) * page, first_pos)
                half = jnp.where(here, lax.rem(bid, blocks_per_page), half)
            kpos = first_pos + in_page
            mask = (in_page // SEL_BLOCK == half) & (kpos <= tpos)
            s = jnp.where(mask, _dot(qg, kt), NEG)
            e = jnp.where(mask, jnp.exp(s - jnp.max(s, axis=-1, keepdims=True)), 0.0)
            p = e / jnp.sum(e, axis=-1, keepdims=True)
            o_sel = o_sel + jnp.where(trow == t, _dot_nt(p.astype(BF16), vt), 0.0)
        for r in range(NSA_REP):
            h = g * NSA_REP + r
            hs = slice(h * HEAD_DIM, (h + 1) * HEAD_DIM)
            outs.append(o_cmp[:, hs] * gates[:, 3 * h:3 * h + 1] + o_sel[r * ts:(r + 1) * ts] * gates[:, 3 * h + 1:3 * h + 2]
                        + o_win[:, hs] * gates[:, 3 * h + 2:3 * h + 3])
    o_ref[...] = jnp.concatenate(outs, axis=1)


def _nsa_sample_sel(idx_flat, page_table_flat, pool_slabs, new_slabs, q, o_cmp, o_win, gates, *, b, past, base, tv):
    nsel = idx_flat.shape[0] // (b * NSA_KV_HEADS * TS)
    page = pool_slabs.shape[2]
    blk3 = lambda a: pl.BlockSpec((None,) + a.shape[1:], lambda bb, i_, p_: (bb, 0, 0))
    return pl.pallas_call(
        functools.partial(_nsa_sel_kernel, past=past, base=base, n_pages=page_table_flat.shape[0] // b,
                          tv=tv, nsel=nsel),
        out_shape=jax.ShapeDtypeStruct((b, TS, 512), F32),
        grid_spec=pltpu.PrefetchScalarGridSpec(
            num_scalar_prefetch=2,
            grid=(b,),
            in_specs=[pl.BlockSpec(memory_space=pl.ANY), pl.BlockSpec(memory_space=pl.ANY),
                      blk3(q), blk3(o_cmp), blk3(o_win), blk3(gates)],
            out_specs=pl.BlockSpec((None, TS, 512), lambda bb, i_, p_: (bb, 0, 0)),
            scratch_shapes=[pltpu.VMEM((2, 2, NSA_KV_HEADS * tv, HEAD_DIM, nsel * page), F32),
                            pltpu.SemaphoreType.DMA((2,))]),
        compiler_params=_cparams(("arbitrary",)),
        name="nsa_sample_sel",
    )(idx_flat, page_table_flat, pool_slabs, new_slabs, q, o_cmp, o_win, gates)


def _layer_weights(l, norm_mix, w_in, gla_wa2, gla_ba, gla_norm, cmp_pe, cmp_w1, cmp_w2, w_out, norm_mem,
                   w_mq, w_mkv, w_mo, norm_ffn, w_ff1, w_ff2):
    pe_big, w1_big, w2_big = _pack_compress(cmp_pe[l], cmp_w1[l], cmp_w2[l])
    return dict(
        pages=_pack_compress_pages(cmp_pe[l], cmp_w1[l], cmp_w2[l]), norm_mix=norm_mix[l][None, :], w_in=_pack_w_in(w_in[l]), wa2=_pack_wa2(gla_wa2[l]),
        ba=gla_ba[l][None, :], gnorm=gla_norm[l][None, :], pe_big=pe_big, w1_big=w1_big, w2_big=w2_big,
        w_out=w_out[l].astype(BF16), norm_mem=norm_mem[l][None, :], w_mq=w_mq[l].astype(BF16),
        w_mkv=w_mkv[l].astype(BF16), w_mo=w_mo[l].astype(BF16), norm_ffn=norm_ffn[l][None, :],
        w_ff1=w_ff1[l].astype(BF16), w_ff2=w_ff2[l].astype(BF16))


def _even_odd(kcv, b):
    nb = kcv.shape[0] // b
    return kcv.reshape(b, nb // 2, 2, -1).transpose(0, 2, 1, 3).reshape(b * nb, -1)


def _prompt_layer(x, mem, p, tabs, gfinal, *, b, t, final_norm):
    q, cmp_rows, slc_rows, win_rows, gates, gla, ret = _project(
        x, p['norm_mix'], p['w_in'], p['wa2'], p['ba'], tabs, tm=256)
    kcv = _compress(cmp_rows.reshape(-1, CMP_BLOCK * KV_W), p['pe_big'], p['w1_big'], p['w2_big'], tr=128)
    o_nsa = _nsa_prompt(q, _even_odd(kcv, b), slc_rows, win_rows, gates, b=b, t=t, tq=128, tk=256)
    sg0 = jnp.zeros((b, GLA_HEADS, GLA_DK, GLA_DV), F32)
    sr0 = jnp.zeros((b, RET_HEADS, RET_DIM, RET_DIM), F32)
    o_rec, s_gla, s_ret = _recurrent(gla, ret, p['gnorm'], sg0, sr0, b=b, t=t, tt=128)
    x = _outproj(x, o_nsa, o_rec, p['w_out'], tm=512)
    mem_kv = _matmul(mem, p['w_mkv'], tm=256)
    x = _cross(x, p['norm_mem'], p['w_mq'], p['w_mo'], mem_kv, b=b, t=t, tq=512)
    x = _ffn(x, p['norm_ffn'], p['w_ff1'], p['w_ff2'], gfinal, tm=1024, tf=1024, final_norm=final_norm)
    wlen = min(WINDOW, t)
    win_tail = win_rows.reshape(b, t, KV_W)[:, t - wlen:]
    return x, (cmp_rows, slc_rows, win_tail, s_gla, s_ret, mem_kv)


def _sample_layer(x, p, tabs, gfinal, cmp_slabs, slc_slabs, win_slabs, sg0, sr0, mem_kv, pt_flat,
                  *, b, past, tv, base, final_norm):
    n = b * TS
    q, cmp_rows, slc_rows, win_rows, gates, gla, ret = _project(
        x, p['norm_mix'], p['w_in'], p['wa2'], p['ba'], tabs, tm=n, valid_period=(TS, tv))
    rows3 = lambda a: a.reshape(b, TS, a.shape[-1])
    pad_blk = lambda a: jnp.pad(rows3(a), ((0, 0), (0, SEL_BLOCK - TS), (0, 0)))
    n_new_blk = SEL_BLOCK // CMP_BLOCK
    pg = min(64, past // PAGE_SIZE)
    kc_past = _compress_pages(pt_flat, cmp_slabs, *p['pages'], base=base, pg=pg)
    kc_new = _compress(pad_blk(cmp_rows).reshape(b * n_new_blk, CMP_BLOCK * KV_W), p['pe_big'], p['w1_big'],
                       p['w2_big'], tr=b * n_new_blk)
    kc_new = jnp.pad(kc_new.reshape(b, n_new_blk, KV_W), ((0, 0), (0, TS - n_new_blk), (0, 0)))
    q3 = rows3(q)
    o_cmp, o_win, idx, new_win = _nsa_sample_a(q3, kc_past, kc_new, win_slabs, rows3(win_rows), b=b, past=past,
                                               n_new_blk=n_new_blk, tv=tv)
    nsel = min(N_SELECT, past // SEL_BLOCK + 1)
    slc_new = jnp.pad(jnp.swapaxes(rows3(slc_rows), 1, 2), ((0, 0), (0, 0), (0, cmp_slabs.shape[2] - TS)))
    o_nsa = _nsa_sample_sel(idx[..., :nsel].reshape(-1), pt_flat, slc_slabs, slc_new, q3, o_cmp, o_win,
                            rows3(gates), b=b, past=past, base=base, tv=tv)
    o_rec, s_gla, s_ret = _recurrent(gla, ret, p['gnorm'], sg0, sr0, b=b, t=TS, tt=TS, n_valid=tv)
    x = _outproj(x, o_nsa.reshape(n, -1), o_rec, p['w_out'], tm=n)
    x = _cross(x, p['norm_mem'], p['w_mq'], p['w_mo'], mem_kv, b=b, t=TS, tq=TS)
    x = _ffn(x, p['norm_ffn'], p['w_ff1'], p['w_ff2'], gfinal, tm=n, tf=1024, final_norm=final_norm)
    return x, (rows3(cmp_rows)[:, :tv], rows3(slc_rows)[:, :tv], new_win, s_gla, s_ret)


def kernel(x_prompt, x_sample, mem_prompt, cache_cmp_kv, cache_slc_kv, cache_win_kv, state_gla, state_ret,
           cache_mem_kv, page_table, norm_mix, w_in, gla_wa2, gla_ba, gla_norm, cmp_pe, cmp_w1, cmp_w2, w_out,
           norm_mem, w_mq, w_mkv, w_mo, norm_ffn, w_ff1, w_ff2, norm_final):
    depth = w_in.shape[0]
    b, t, d = x_prompt.shape
    g, hd = NSA_KV_HEADS, HEAD_DIM
    gfinal = norm_final[None, :]
    tabs_p = _all_rope_tables(jnp.arange(t))
    hp = x_prompt.reshape(b * t, d)
    mem = mem_prompt.reshape(-1, d)
    sb, tv = x_sample.shape[:2]
    assert tv <= TS
    past = page_table.shape[1] * PAGE_SIZE
    tabs_s = _all_rope_tables(past + jnp.arange(sb * TS) % TS)
    hs = jnp.pad(x_sample, ((0, 0), (0, TS - tv), (0, 0))).reshape(sb * TS, d)
    pt_flat = page_table.reshape(-1)
    n_pool = cache_cmp_kv.shape[1]
    cmp_slabs = _pool_slabs(cache_cmp_kv)
    slc_slabs = _pool_slabs(cache_slc_kv)
    win_slabs = jnp.transpose(cache_win_kv, (0, 1, 3, 4, 5, 2)).reshape(depth, sb, KV_W, -1)
    new_p = [[] for _ in range(6)]
    new_s = [[] for _ in range(5)]
    for l in range(depth):
        p = _layer_weights(l, norm_mix, w_in, gla_wa2, gla_ba, gla_norm, cmp_pe, cmp_w1, cmp_w2, w_out, norm_mem,
                           w_mq, w_mkv, w_mo, norm_ffn, w_ff1, w_ff2)
        last = l == depth - 1
        hp, st = _prompt_layer(hp, mem, p, tabs_p, gfinal, b=b, t=t, final_norm=last)
        for lst, a in zip(new_p, st):
            lst.append(a)
        hs, st = _sample_layer(hs, p, tabs_s, gfinal, cmp_slabs, slc_slabs, win_slabs[l], state_gla[l], state_ret[l],
                               cache_mem_kv[l].reshape(-1, 2 * D_MODEL), pt_flat, b=sb, past=past, tv=tv,
                               base=l * n_pool, final_norm=last)
        for lst, a in zip(new_s, st):
            lst.append(a)
    cmp_p, slc_p, win_p, gla_p, ret_p, mem_p = [jnp.stack(a) for a in new_p]
    cmp_s, slc_s, win_s, gla_s, ret_s = [jnp.stack(a) for a in new_s]
    n_mem = mem_prompt.shape[1]
    kv6 = lambda a: a.reshape(a.shape[:3] + (2, g, hd))
    win_s = jnp.transpose(win_s.reshape(depth, sb, 2, g, hd, -1), (0, 1, 5, 2, 3, 4))
    return (hp.reshape(b, t, d), hs.reshape(sb, TS, d)[:, :tv],
            kv6(cmp_p.reshape(depth, b, t, KV_W)), kv6(slc_p.reshape(depth, b, t, KV_W)), kv6(win_p), gla_p, ret_p,
            mem_p.reshape(depth, b, n_mem, 2, MEM_HEADS, MEM_HD),
            kv6(cmp_s), kv6(slc_s), win_s, gla_s, ret_s)
```

```python
import functools
import math

import numpy as np
import jax
import jax.numpy as jnp
from jax import lax
from jax.experimental import pallas as pl
from jax.experimental.pallas import tpu as pltpu

F32 = jnp.float32
BF16 = jnp.bfloat16

D_MODEL = 1024
PAGE_SIZE = 128
HEAD_DIM = 64
NSA_HEADS = 8
NSA_KV_HEADS = 2
NSA_REP = NSA_HEADS // NSA_KV_HEADS
CMP_BLOCK = 32
SEL_BLOCK = 64
N_SELECT = 16
WINDOW = 512
ROT_DIM = HEAD_DIM // 4
ROPE_THETA = 500000.0
GLA_HEADS = 4
GLA_DK = 32
GLA_DV = 64
GLA_RANK = 16
GLA_TAU = 16.0
RET_HEADS = 4
RET_DIM = 64
RET_THETA = 10000.0
MEM_HEADS = 4
MEM_HD = D_MODEL // MEM_HEADS
D_FF = 4 * D_MODEL
EPS = 1e-6
IN_SIZES = (NSA_HEADS * HEAD_DIM, 6 * NSA_KV_HEADS * HEAD_DIM, 3 * NSA_HEADS,
            GLA_HEADS * GLA_DK, GLA_HEADS * GLA_DK, GLA_HEADS * GLA_DV, GLA_RANK, GLA_HEADS * GLA_DV,
            RET_HEADS * RET_DIM, RET_HEADS * RET_DIM, RET_HEADS * RET_DIM, RET_HEADS * RET_DIM)

LANES = 128
VMEM_LIMIT = 56 << 20
NEG = -1e30
M_INIT = -1e29
KV_W = 2 * NSA_KV_HEADS * HEAD_DIM

C_Q = 0
C_KV = 512
C_GLA = 1280
C_RET = 2048
C_TAIL = 3072
W_IN_COLS = 3200
GLA_W = 896
RET_W = 1024


def _cparams(sem, vmem=VMEM_LIMIT):
    return pltpu.CompilerParams(dimension_semantics=sem, vmem_limit_bytes=vmem)


def _dot(a, b):
    return jnp.dot(a, b, preferred_element_type=F32)


def _dot_nt(a, b):
    return lax.dot_general(a, b, (((1,), (1,)), ((), ())), preferred_element_type=F32)


def _dot_tn(a, b):
    return lax.dot_general(a, b, (((0,), (0,)), ((), ())), preferred_element_type=F32)


def _split3(x):
    hi = x.astype(BF16)
    r = x - hi.astype(F32)
    mid = r.astype(BF16)
    lo = (r - mid.astype(F32)).astype(BF16)
    return hi, mid, lo


def _rms(x, eps=EPS):
    return x * lax.rsqrt(jnp.mean(x * x, axis=-1, keepdims=True) + eps)


def _silu(x):
    return x * jax.nn.sigmoid(x)


def _rope128(v, c, sa, sb, half):
    return v * c + pltpu.roll(v, LANES - half, 1) * sa + pltpu.roll(v, half, 1) * sb


def _proj_kernel(x_ref, gain_ref, w_ref, wa2_ref, ba_ref, nc_ref, nsa_ref, nsb_ref, rc_ref, rsa_ref, rsb_ref,
                 q_ref, cmp_ref, slc_ref, win_ref, gate_ref, gla_ref, ret_ref, *, valid_period):
    x = x_ref[...]
    xn = _rms(x) * gain_ref[...]
    if valid_period is not None:
        period, n_valid = valid_period
        row = lax.broadcasted_iota(jnp.int32, (x.shape[0], 1), 0)
        xn = jnp.where(lax.rem(row, period) < n_valid, xn, 0.0)
    xb = xn.astype(BF16)

    def mm(a, b):
        return _dot(xb, w_ref[:, a:b])

    nc, nsa, nsb = nc_ref[...], nsa_ref[...], nsb_ref[...]
    rc, rsa, rsb = rc_ref[...], rsa_ref[...], rsb_ref[...]
    half_n = ROT_DIM // 2
    half_r = RET_DIM // 2

    q = mm(C_Q, C_KV)
    for c in range(4):
        sl = slice(c * LANES, (c + 1) * LANES)
        q_ref[:, sl] = (_rope128(q[:, sl], nc, nsa, nsb, half_n) * (HEAD_DIM ** -0.5)).astype(q_ref.dtype)

    kv = mm(C_KV, C_GLA)
    for br, ref in enumerate((cmp_ref, slc_ref, win_ref)):
        ref[:, 0:LANES] = _rope128(kv[:, br * KV_W:br * KV_W + LANES], nc, nsa, nsb, half_n)
        ref[:, LANES:KV_W] = kv[:, br * KV_W + LANES:(br + 1) * KV_W]

    tail = mm(C_TAIL, W_IN_COLS)
    gate_ref[...] = jax.nn.sigmoid(tail)
    z = jnp.dot(tail, wa2_ref[...], preferred_element_type=F32, precision=lax.Precision.HIGHEST) + ba_ref[...]
    log_a = (jnp.minimum(z, 0.0) - jnp.log1p(jnp.exp(-jnp.abs(z)))) * (1.0 / GLA_TAU)

    gl = mm(C_GLA, C_RET)
    gla_ref[:, 0:128] = gl[:, 0:128] * (GLA_DK ** -0.5)
    gla_ref[:, 128:256] = gl[:, 128:256]
    gla_ref[:, 256:384] = log_a
    gla_ref[:, 384:640] = gl[:, 256:512]
    gla_ref[:, 640:896] = _silu(gl[:, 512:768])

    rt = mm(C_RET, C_TAIL)
    for c in range(2):
        sl = slice(c * LANES, (c + 1) * LANES)
        ret_ref[:, sl] = _rope128(rt[:, sl], rc, rsa, rsb, half_r)
        sk = slice(256 + c * LANES, 256 + (c + 1) * LANES)
        ret_ref[:, sk] = _rope128(rt[:, sk], rc, rsa, rsb, half_r) * (RET_DIM ** -0.5)
    ret_ref[:, 512:768] = rt[:, 512:768]
    ret_ref[:, 768:1024] = _silu(rt[:, 768:1024])


def _project(x, gain, w, wa2, ba, tabs, *, tm, valid_period=None):
    n = x.shape[0]
    p = tabs[0].shape[0]
    nt = p // tm
    row = lambda w_: pl.BlockSpec((tm, w_), lambda i: (i, 0))
    full = lambda a: pl.BlockSpec(a.shape, lambda i: (0, 0))
    tab = pl.BlockSpec((tm, LANES), lambda i: (i % nt, 0))
    out_shape = (jax.ShapeDtypeStruct((n, 512), BF16),
                 jax.ShapeDtypeStruct((n, KV_W), F32), jax.ShapeDtypeStruct((n, KV_W), F32),
                 jax.ShapeDtypeStruct((n, KV_W), F32), jax.ShapeDtypeStruct((n, LANES), F32),
                 jax.ShapeDtypeStruct((n, GLA_W), F32), jax.ShapeDtypeStruct((n, RET_W), F32))
    return pl.pallas_call(
        functools.partial(_proj_kernel, valid_period=valid_period),
        out_shape=out_shape,
        grid=(n // tm,),
        in_specs=[row(D_MODEL), full(gain), full(w), full(wa2), full(ba)] + [tab] * 6,
        out_specs=(row(512), row(KV_W), row(KV_W), row(KV_W), row(LANES), row(GLA_W), row(RET_W)),
        compiler_params=_cparams(("parallel",)),
        name="proj",
    )(x, gain, w, wa2, ba, *tabs)


def _rope_tables(pos, inv_freq):
    nf = inv_freq.shape[0]
    ang = pos.astype(F32)[:, None] * inv_freq[None, :]
    cos, sin = jnp.cos(ang), jnp.sin(ang)
    ones = jnp.ones((pos.shape[0], HEAD_DIM - 2 * nf), F32)
    zeros = jnp.zeros((pos.shape[0], HEAD_DIM - nf), F32)
    c = jnp.concatenate([cos, cos, ones], axis=1)
    sa = jnp.concatenate([-sin, zeros], axis=1)
    sb = jnp.concatenate([zeros[:, :nf], sin, zeros[:, :HEAD_DIM - 2 * nf]], axis=1)
    return tuple(jnp.tile(t, (1, LANES // HEAD_DIM)) for t in (c, sa, sb))


def _all_rope_tables(pos):
    nsa_f = ROPE_THETA ** (-jnp.arange(0, ROT_DIM, 2, dtype=F32) / ROT_DIM)
    ret_f = RET_THETA ** (-jnp.linspace(0.0, 1.0, RET_DIM // 2, dtype=F32))
    return _rope_tables(pos, nsa_f) + _rope_tables(pos, ret_f)


def _pack_w_in(w):
    offs = np.cumsum((0,) + IN_SIZES)
    nq, nkv, ngate, gq, gk, gv, ga, gr, rq, rk, rv, rg = [w[:, offs[i]:offs[i + 1]] for i in range(12)]
    pad = jnp.zeros((w.shape[0], W_IN_COLS - C_TAIL - ngate.shape[1] - ga.shape[1]), w.dtype)
    return jnp.concatenate([nq, nkv, gq, gk, gv, gr, rq, rk, rv, rg, ngate, ga, pad], axis=1).astype(BF16)


def _pack_wa2(wa2):
    top = jnp.zeros((3 * NSA_HEADS, wa2.shape[1]), wa2.dtype)
    bot = jnp.zeros((LANES - 3 * NSA_HEADS - GLA_RANK, wa2.shape[1]), wa2.dtype)
    return jnp.concatenate([top, wa2, bot], axis=0)


def _gelu_tanh(x):
    return 0.5 * x * (1.0 + jnp.tanh(0.7978845608028654 * (x + 0.044715 * (x * x * x))))


def _compress_kernel(x_ref, pe_ref, w1_ref, w2_ref, o_ref):
    xb = (x_ref[...] + pe_ref[...]).astype(BF16)
    h = _gelu_tanh(_dot(xb, w1_ref[...]))
    o_ref[...] = _dot(h.astype(BF16), w2_ref[...])


def _compress(rows_flat, pe_big, w1_big, w2_big, *, tr):
    r, k = rows_flat.shape
    tr = min(tr, r)
    full = lambda a: pl.BlockSpec(a.shape, lambda i: (0, 0))
    return pl.pallas_call(
        _compress_kernel,
        out_shape=jax.ShapeDtypeStruct((r, KV_W), F32),
        grid=(r // tr,),
        in_specs=[pl.BlockSpec((tr, k), lambda i: (i, 0)), full(pe_big), full(w1_big), full(w2_big)],
        out_specs=pl.BlockSpec((tr, KV_W), lambda i: (i, 0)),
        compiler_params=_cparams(("parallel",)),
        name="compress",
    )(rows_flat, pe_big, w1_big, w2_big)


def _pack_compress(pe, w1, w2):
    g, d = NSA_KV_HEADS, HEAD_DIM
    hid = w1.shape[-1]
    pe_big = jnp.broadcast_to(pe.transpose(1, 0, 2)[:, :, None, :], (CMP_BLOCK, 2, g, d)).reshape(1, -1)
    eye = jnp.eye(2 * g, dtype=w1.dtype).reshape(2, g, 2, g)
    w1r = w1.reshape(2, CMP_BLOCK, d, hid)
    w1_big = jnp.einsum('kjdc,kgKG->jkgdKGc', w1r, eye).reshape(CMP_BLOCK * KV_W, 2 * g * hid)
    w2_big = jnp.einsum('kce,kgKG->kgcKGe', w2, eye).reshape(2 * g * hid, KV_W)
    return pe_big, w1_big.astype(BF16), w2_big.astype(BF16)


def _stack_heads(q, g):
    return jnp.concatenate([q[:, (g * NSA_REP + r) * HEAD_DIM:(g * NSA_REP + r + 1) * HEAD_DIM]
                            for r in range(NSA_REP)], axis=0)


def _flash_init(n_chains, nq):
    return tuple((jnp.full((1, nq), M_INIT, F32), jnp.zeros((2 * HEAD_DIM, nq), F32)) for _ in range(n_chains))


def _flash_finish(carry):
    return [acc[0:HEAD_DIM] / acc[HEAD_DIM:HEAD_DIM + 1] for _, acc in carry]


def _flash_t(chains, tpos, lo, hi, tk, carry):
    def body(kt, carry):
        k0 = pl.multiple_of(kt * tk, tk)
        kpos = k0 + lax.broadcasted_iota(jnp.int32, (tk, 1), 0)
        ones = jnp.ones((tk, HEAD_DIM), BF16)
        out = []
        for (kv_ref, kcol, vcol, q_t, mask_fn), (m_old, acc) in zip(chains, carry):
            kb = kv_ref[pl.ds(k0, tk), kcol].astype(BF16)
            vb = jnp.concatenate([kv_ref[pl.ds(k0, tk), vcol].astype(BF16), ones], axis=1)
            s_t = mask_fn(_dot(kb, q_t), k0, kpos, tpos)
            m_new = jnp.maximum(m_old, jnp.max(s_t, axis=0, keepdims=True))
            p = jnp.exp((s_t - m_new).astype(BF16))
            out.append((m_new, jnp.exp(m_old - m_new) * acc + _dot_tn(vb, p)))
        return tuple(out)

    return lax.fori_loop(lo, hi, body, carry)


def _nsa_prompt_kernel(q_ref, kcv_ref, slc_ref, win_ref, gate_ref, exp_ref, o_ref, selx_ref, *, tq, tk):
    i = pl.program_id(1)
    mq = NSA_REP * tq
    nb = kcv_ref.shape[0]
    nsb = nb // 2
    q_t = q_ref[...].astype(F32).T
    gate_t = gate_ref[...].T
    tcol = i * tq + lax.rem(lax.broadcasted_iota(jnp.int32, (1, mq), 1), tq)
    tq_col = i * tq + lax.broadcasted_iota(jnp.int32, (1, tq), 1)
    blk_row = lax.broadcasted_iota(jnp.int32, (nb, 1), 0)
    blk = jnp.where(blk_row < nsb, 2 * blk_row, 2 * (blk_row - nsb) + 1)
    cmask = (blk + 1) * CMP_BLOCK - 1 <= tcol
    jrow = lax.broadcasted_iota(jnp.int32, (nsb, 1), 0)
    cur = tq_col // SEL_BLOCK
    forced = (jrow == 0) | (jrow == cur) | (jrow == cur - 1)
    valid = jrow * SEL_BLOCK <= tq_col
    n_kt = ((i + 1) * tq + tk - 1) // tk
    win_lo = jnp.maximum(i * tq - WINDOW + 1, 0) // tk

    def sel_mask(g, causal):
        def fn(s, k0, kpos, tpos):
            bias = selx_ref[g, pl.ds(k0, tk), :]
            s = s + jnp.concatenate([bias] * NSA_REP, axis=1)
            return jnp.where(kpos <= tpos, s, NEG) if causal else s
        return fn

    def win_mask(s, k0, kpos, tpos):
        return jnp.where(kpos <= tpos, jnp.where(kpos > tpos - WINDOW, s, NEG), NEG)

    kcols = [slice(g * HEAD_DIM, (g + 1) * HEAD_DIM) for g in range(NSA_KV_HEADS)]
    vcols = [slice((NSA_KV_HEADS + g) * HEAD_DIM, (NSA_KV_HEADS + g + 1) * HEAD_DIM) for g in range(NSA_KV_HEADS)]
    qgs, o_cmps = [], []
    for g in range(NSA_KV_HEADS):
        qg = jnp.concatenate([q_t[(g * NSA_REP + r) * HEAD_DIM:(g * NSA_REP + r + 1) * HEAD_DIM, :]
                              for r in range(NSA_REP)], axis=1).astype(BF16)
        qgs.append(qg)
        kcol, vcol = kcols[g], vcols[g]

        s_t = _dot(kcv_ref[:, kcol].astype(BF16), qg)
        sm = jnp.where(cmask, s_t, NEG)
        mx = jnp.max(sm, axis=0, keepdims=True)
        mx = jnp.where(mx > 0.5 * NEG, mx, 0.0)
        e = jnp.where(cmask, jnp.exp(sm - mx), 0.0)
        den = jnp.sum(e, axis=0, keepdims=True)
        p = e / jnp.where(den > 0.0, den, 1.0)
        o_cmp = _dot_tn(kcv_ref[:, vcol].astype(BF16), p.astype(BF16))
        ps = p[:, 0:tq]
        for r in range(1, NSA_REP):
            ps = ps + p[:, r * tq:(r + 1) * tq]
        imp = ps[0:nsb] + ps[nsb:nb]

        score = jnp.where(forced, -NEG, jnp.where(valid, imp, NEG))
        rank = jnp.zeros((nsb, tq), F32)
        for c in range(nsb):
            sc = score[c:c + 1, :]
            beats = (sc > score) | ((sc == score) & (jrow > c))
            rank = rank + jnp.where(beats, 1.0, 0.0)
        sel = jnp.where(rank < float(N_SELECT), 1.0, 0.0).astype(BF16)
        selx_ref[g] = (_dot(exp_ref[...], sel) - 1.0) * (-NEG)
        o_cmps.append(o_cmp)

    groups = range(NSA_KV_HEADS)
    carry = _flash_t([(slc_ref, kcols[g], vcols[g], qgs[g], sel_mask(g, False)) for g in groups], tcol,
                     0, win_lo, tk, _flash_init(NSA_KV_HEADS, mq))
    carry = _flash_t([(slc_ref, kcols[g], vcols[g], qgs[g], sel_mask(g, True)) for g in groups]
                     + [(win_ref, kcols[g], vcols[g], qgs[g], win_mask) for g in groups], tcol,
                     win_lo, n_kt, tk, carry + _flash_init(NSA_KV_HEADS, mq))
    o_sels = _flash_finish(carry[:NSA_KV_HEADS])
    o_wins = _flash_finish(carry[NSA_KV_HEADS:])

    for g in groups:
        outs = []
        for r in range(NSA_REP):
            h = g * NSA_REP + r
            cs = slice(r * tq, (r + 1) * tq)
            outs.append(o_cmps[g][:, cs] * gate_t[3 * h:3 * h + 1, :]
                        + o_sels[g][:, cs] * gate_t[3 * h + 1:3 * h + 2, :]
                        + o_wins[g][:, cs] * gate_t[3 * h + 2:3 * h + 3, :])
        o_ref[:, g * NSA_REP * HEAD_DIM:(g + 1) * NSA_REP * HEAD_DIM] = jnp.concatenate(outs, axis=0).T


def _nsa_prompt(q, kcv, slc, win, gates, *, b, t, tq, tk):
    n = b * t
    nq = t // tq
    nb = t // CMP_BLOCK
    nsb = t // SEL_BLOCK
    tk = min(tk, t)
    expand = (np.arange(t)[:, None] // SEL_BLOCK == np.arange(nsb)[None, :]).astype(np.float32)
    expand = jnp.asarray(expand, BF16)
    return pl.pallas_call(
        functools.partial(_nsa_prompt_kernel, tq=tq, tk=tk),
        out_shape=jax.ShapeDtypeStruct((n, NSA_HEADS * HEAD_DIM), F32),
        grid=(b, nq),
        in_specs=[pl.BlockSpec((tq, 512), lambda bb, i: (bb * nq + i, 0)),
                  pl.BlockSpec((nb, KV_W), lambda bb, i: (bb, 0)),
                  pl.BlockSpec((t, KV_W), lambda bb, i: (bb, 0)),
                  pl.BlockSpec((t, KV_W), lambda bb, i: (bb, 0)),
                  pl.BlockSpec((tq, LANES), lambda bb, i: (bb * nq + i, 0)),
                  pl.BlockSpec((t, nsb), lambda bb, i: (0, 0))],
        out_specs=pl.BlockSpec((tq, 512), lambda bb, i: (bb * nq + i, 0)),
        scratch_shapes=[pltpu.VMEM((NSA_KV_HEADS, t, tq), F32)],
        compiler_params=_cparams(("parallel", "arbitrary")),
        name="nsa_prompt",
    )(q, kcv, slc, win, gates, expand)


def _gla_level_matrix(tt):
    lv = int(math.log2(tt))
    m = np.zeros((lv, tt, tt), np.float32)
    t = np.arange(tt)
    for l in range(lv):
        half = 1 << l
        split = ((t >> (l + 1)) << (l + 1)) + half - 1
        u = np.arange(tt)[None, :]
        upper = t > split
        m[l] = np.where(upper[:, None], (u > split[:, None]) & (u <= t[:, None]),
                        (u > t[:, None]) & (u <= split[:, None]))
    return m.reshape(lv * tt, tt)


def _ret_log_decay():
    return [float(np.log(np.float32(1.0) - np.float32(2.0) ** np.float32(-5.0 - h))) for h in range(RET_HEADS)]


def _recur_kernel(gla_ref, ret_ref, gn_ref, lvl_ref, tril_ref, sg0_ref, sr0_ref,
                  o_ref, sg_out_ref, sr_out_ref, sg_ref, sr_ref, *, tt, n_valid):
    ti = pl.program_id(1)
    nt = pl.num_programs(1)

    @pl.when(ti == 0)
    def _():
        sg_ref[...] = sg0_ref[...]
        sr_ref[...] = sr0_ref[...]

    for bi in range(gla_ref.shape[0]):
        _recur_tile(gla_ref.at[bi], ret_ref.at[bi], gn_ref, lvl_ref, tril_ref, o_ref.at[bi], sg_ref.at[bi],
                    sr_ref.at[bi], tt=tt, n_valid=n_valid)

    @pl.when(ti == nt - 1)
    def _():
        sg_out_ref[...] = sg_ref[...]
        sr_out_ref[...] = sr_ref[...]


def _recur_tile(gla_ref, ret_ref, gn_ref, lvl_ref, tril_ref, o_ref, sg_ref, sr_ref, *, tt, n_valid):
    levels = int(math.log2(tt))
    rowi = lax.broadcasted_iota(jnp.int32, (tt, 1), 0)
    coli = lax.broadcasted_iota(jnp.int32, (1, tt), 1)

    gq = gla_ref[:, 0:128]
    gk = gla_ref[:, 128:256]
    la = gla_ref[:, 256:384]
    gv = gla_ref[:, 384:640]
    gr = gla_ref[:, 640:896]
    if n_valid < tt:
        la = jnp.where(rowi < n_valid, la, 0.0)
    parts = _split3(la)
    tril = tril_ref[...]
    lvl = lvl_ref[...]
    ones = jnp.ones((tt, GLA_DV), BF16)
    cum = sum(_dot(tril, pt) for pt in parts)
    dlv = sum(_dot(lvl, pt) for pt in parts)
    tot = sum(_dot_tn(pt, ones) for pt in parts)
    q_dec = gq * jnp.exp(cum)
    k_dec = gk * jnp.exp(cum[tt - 1:tt, :] - cum)
    gla_out = []
    for h in range(GLA_HEADS):
        sl = slice(h * GLA_DK, (h + 1) * GLA_DK)
        vs = slice(h * GLA_DV, (h + 1) * GLA_DV)
        qh, kh = gq[:, sl], gk[:, sl]
        vh = gv[:, vs].astype(BF16)
        attn = jnp.where(rowi == coli, _dot_nt(qh.astype(BF16), kh.astype(BF16)), 0.0)
        for l in range(levels):
            ed = jnp.exp(dlv[l * tt:(l + 1) * tt, sl])
            upper = ((rowi >> l) & 1) == 1
            qe = jnp.where(upper, qh * ed, 0.0).astype(BF16)
            ke = jnp.where(upper, 0.0, kh * ed).astype(BF16)
            same = (rowi >> (l + 1)) == (coli >> (l + 1))
            attn = attn + jnp.where(same, _dot_nt(qe, ke), 0.0)
        s_h = sg_ref[h]
        o = _dot(attn.astype(BF16), vh) + _dot(q_dec[:, sl].astype(BF16), s_h.astype(BF16))
        sg_ref[h] = jnp.exp(tot[sl, :]) * s_h + _dot_tn(k_dec[:, sl].astype(BF16), vh)
        gla_out.append(_rms(o) * gn_ref[...] * gr[:, vs])
    o_ref[:, 0:256] = jnp.concatenate(gla_out, axis=1)

    rel = (rowi - coli).astype(F32)
    pos1 = (rowi + 1).astype(F32)
    left = (n_valid - 1 - rowi).astype(F32)
    ret_out = []
    for h, lg in enumerate(_ret_log_decay()):
        sl = slice(h * RET_DIM, (h + 1) * RET_DIM)
        qh = ret_ref[:, sl]
        kh = ret_ref[:, 256 + h * RET_DIM:256 + (h + 1) * RET_DIM]
        vh = ret_ref[:, 512 + h * RET_DIM:512 + (h + 1) * RET_DIM].astype(BF16)
        gh = ret_ref[:, 768 + h * RET_DIM:768 + (h + 1) * RET_DIM]
        dmat = jnp.where(rel >= 0.0, jnp.exp(lg * jnp.maximum(rel, 0.0)), 0.0)
        attn = _dot_nt(qh.astype(BF16), kh.astype(BF16)) * dmat
        s_h = sr_ref[h]
        o = _dot(attn.astype(BF16), vh) + _dot(qh.astype(BF16), s_h.astype(BF16)) * jnp.exp(lg * pos1)
        sr_ref[h] = math.exp(lg * n_valid) * s_h + _dot_tn((kh * jnp.exp(lg * left)).astype(BF16), vh)
        ret_out.append(_rms(o) * gh)
    o_ref[:, 256:512] = jnp.concatenate(ret_out, axis=1)


def _recurrent(gla, ret, gnorm, sg0, sr0, *, b, t, tt, nbk, n_valid=None):
    nt = t // tt
    n_valid = tt if n_valid is None else n_valid
    assert n_valid == tt or nt == 1
    lvl = jnp.asarray(_gla_level_matrix(tt), BF16)
    tril = jnp.asarray(np.tril(np.ones((tt, tt), np.float32)), BF16)
    full2 = lambda a: pl.BlockSpec(a.shape, lambda bb, i: (0, 0))
    st = lambda a: pl.BlockSpec((nbk,) + a.shape[1:], lambda bb, i: (bb, 0, 0, 0))
    tok = lambda w: pl.BlockSpec((nbk, tt, w), lambda bb, i: (bb, i, 0))
    out, s_gla, s_ret = pl.pallas_call(
        functools.partial(_recur_kernel, tt=tt, n_valid=n_valid),
        out_shape=(jax.ShapeDtypeStruct((b, t, 512), F32), jax.ShapeDtypeStruct(sg0.shape, F32),
                   jax.ShapeDtypeStruct(sr0.shape, F32)),
        grid=(b // nbk, nt),
        in_specs=[tok(GLA_W), tok(RET_W), full2(gnorm), full2(lvl), full2(tril), st(sg0), st(sr0)],
        out_specs=(tok(512), st(sg0), st(sr0)),
        scratch_shapes=[pltpu.VMEM((nbk,) + sg0.shape[1:], F32), pltpu.VMEM((nbk,) + sr0.shape[1:], F32)],
        compiler_params=_cparams(("parallel", "arbitrary")),
        name="recurrent",
    )(gla.reshape(b, t, GLA_W), ret.reshape(b, t, RET_W), gnorm, lvl, tril, sg0, sr0)
    return out.reshape(b * t, 512), s_gla, s_ret


def _outproj_kernel(x_ref, a_ref, b_ref, w_ref, o_ref):
    ka = a_ref.shape[1]
    o_ref[...] = (x_ref[...] + _dot(a_ref[...].astype(BF16), w_ref[0:ka, :])
                  + _dot(b_ref[...].astype(BF16), w_ref[ka:, :]))


def _outproj(x, a, b_, w, *, tm):
    n = x.shape[0]
    row = lambda a_: pl.BlockSpec((tm, a_.shape[1]), lambda i: (i, 0))
    return pl.pallas_call(
        _outproj_kernel,
        out_shape=jax.ShapeDtypeStruct(x.shape, F32),
        grid=(n // tm,),
        in_specs=[row(x), row(a), row(b_), pl.BlockSpec(w.shape, lambda i: (0, 0))],
        out_specs=row(x),
        compiler_params=_cparams(("parallel",)),
        name="outproj",
    )(x, a, b_, w)


def _cross_kernel(x_ref, gain_ref, wq_ref, wo_ref, kv_ref, o_ref):
    x = x_ref[...]
    xb = (_rms(x) * gain_ref[...]).astype(BF16)
    q = _dot(xb, wq_ref[...])
    heads = []
    for h in range(MEM_HEADS):
        qh = q[:, h * MEM_HD:(h + 1) * MEM_HD].astype(BF16)
        if len(kv_ref.shape) == 4:
            kh = kv_ref[:, 0, h, :].astype(BF16)
            vh = kv_ref[:, 1, h, :].astype(BF16)
        else:
            kh = kv_ref[:, h * MEM_HD:(h + 1) * MEM_HD].astype(BF16)
            vh = kv_ref[:, (MEM_HEADS + h) * MEM_HD:(MEM_HEADS + h + 1) * MEM_HD].astype(BF16)
        s = _dot_nt(qh, kh) * (MEM_HD ** -0.5)
        e = jnp.exp(s - jnp.max(s, axis=-1, keepdims=True))
        p = e / jnp.sum(e, axis=-1, keepdims=True)
        heads.append(_dot(p.astype(BF16), vh))
    att = jnp.concatenate(heads, axis=1).astype(BF16)
    o_ref[...] = x + _dot(att, wo_ref[...])


def _cross(x, gain, wq, wo, memkv, *, b, t, tq, layer=None):
    nq = t // tq
    full = lambda a: pl.BlockSpec(a.shape, lambda bb, i: (0, 0))
    row = pl.BlockSpec((tq, D_MODEL), lambda bb, i: (bb * nq + i, 0))
    if layer is None:
        kv_spec = pl.BlockSpec((memkv.shape[0] // b, memkv.shape[1]), lambda bb, i: (bb, 0))
    else:
        kv_spec = pl.BlockSpec((None, None) + memkv.shape[2:], lambda bb, i: (layer, bb, 0, 0, 0, 0))
    return pl.pallas_call(
        _cross_kernel,
        out_shape=jax.ShapeDtypeStruct(x.shape, F32),
        grid=(b, nq),
        in_specs=[row, full(gain), full(wq), full(wo), kv_spec],
        out_specs=row,
        compiler_params=_cparams(("parallel", "arbitrary")),
        name="cross",
    )(x, gain, wq, wo, memkv)


def _ffn_kernel(x_ref, gain_ref, w1_ref, w2_ref, gf_ref, o_ref, xn_ref, acc_ref, *, final_norm):
    j = pl.program_id(1)

    @pl.when(j == 0)
    def _():
        xn_ref[...] = (_rms(x_ref[...]) * gain_ref[...]).astype(BF16)
        acc_ref[...] = x_ref[...]

    h = jnp.maximum(_dot(xn_ref[...], w1_ref[...]), 0.0)
    acc_ref[...] += _dot((h * h).astype(BF16), w2_ref[...])

    @pl.when(j == pl.num_programs(1) - 1)
    def _():
        y = acc_ref[...]
        if final_norm:
            y = _rms(y) * gf_ref[...]
        o_ref[...] = y


def _ffn(x, gain, w1, w2, gfinal, *, tm, tf, final_norm):
    n = x.shape[0]
    row = pl.BlockSpec((tm, D_MODEL), lambda i, j: (i, 0))
    vec = pl.BlockSpec((1, D_MODEL), lambda i, j: (0, 0))
    return pl.pallas_call(
        functools.partial(_ffn_kernel, final_norm=final_norm),
        out_shape=jax.ShapeDtypeStruct(x.shape, F32),
        grid=(n // tm, D_FF // tf),
        in_specs=[row, vec, pl.BlockSpec((D_MODEL, tf), lambda i, j: (0, j)),
                  pl.BlockSpec((tf, D_MODEL), lambda i, j: (j, 0)), vec],
        out_specs=row,
        scratch_shapes=[pltpu.VMEM((tm, D_MODEL), BF16), pltpu.VMEM((tm, D_MODEL), F32)],
        compiler_params=_cparams(("parallel", "arbitrary")),
        name="ffn",
    )(x, gain, w1, w2, gfinal)


def _matmul_kernel(x_ref, w_ref, o_ref):
    o_ref[...] = _dot(x_ref[...].astype(BF16), w_ref[...])


def _matmul(x, w, *, tm):
    n, k = x.shape
    return pl.pallas_call(
        _matmul_kernel,
        out_shape=jax.ShapeDtypeStruct((n, w.shape[1]), F32),
        grid=(n // tm,),
        in_specs=[pl.BlockSpec((tm, k), lambda i: (i, 0)), pl.BlockSpec(w.shape, lambda i: (0, 0))],
        out_specs=pl.BlockSpec((tm, w.shape[1]), lambda i: (i, 0)),
        compiler_params=_cparams(("parallel",)),
        name="memkv",
    )(x, w)


TS = 8


def _pool_slabs(pool):
    depth, n_pool = pool.shape[:2]
    return jnp.transpose(pool, (0, 1, 3, 4, 5, 2)).reshape(depth * n_pool, KV_W, pool.shape[2])


def _cmp_pages_kernel(pt_ref, pool_ref, wexp_hbm, pe_ref, w1_ref, w2_ref, o_ref, buf_ref, wbuf_ref, sem_ref,
                      *, pg, base):
    s = pl.program_id(0)
    ns = pl.num_programs(0)
    slab = pool_ref.shape[1]
    pitch = buf_ref.shape[1] // pg
    n_pairs = wbuf_ref.shape[1]
    blocks_per_page = PAGE_SIZE // CMP_BLOCK

    def page_copy(page, slot, j):
        return pltpu.make_async_copy(pool_ref.at[page], buf_ref.at[slot, pl.ds(j * pitch, slab), :], sem_ref.at[slot])

    def fetch(step, slot):
        def body(j, c):
            page_copy(base + pt_ref[step * pg + j], slot, j).start()
            return c
        lax.fori_loop(0, pg, body, 0)

    @pl.when(s == 0)
    def _():
        weights = pltpu.make_async_copy(wexp_hbm, wbuf_ref, sem_ref.at[2])
        weights.start()
        fetch(0, 0)
        weights.wait()

    @pl.when(s + 1 < ns)
    def _():
        fetch(s + 1, lax.rem(s + 1, 2))

    slot = lax.rem(s, 2)

    def wait_body(j, c):
        page_copy(0, slot, j).wait()
        return c
    lax.fori_loop(0, pg, wait_body, 0)

    for kv in range(2):
        pew = sum(_dot(part, w1_ref[kv]) for part in _split3(pe_ref[kv]))[0:1]
        pew = jnp.concatenate([pew] * blocks_per_page, axis=1)
        for g in range(NSA_KV_HEADS):
            kvg = kv * NSA_KV_HEADS + g

            def body(dd, acc):
                r0 = kvg * HEAD_DIM + 2 * dd
                x0 = buf_ref[slot, pl.ds(r0, pg, stride=pitch), :]
                x1 = buf_ref[slot, pl.ds(r0 + 1, pg, stride=pitch), :]
                x = jnp.concatenate([x0, x1], axis=1).astype(BF16)
                return acc + _dot(x, wbuf_ref[kv, dd])
            acc = lax.fori_loop(0, n_pairs, body, jnp.zeros((pg, w2_ref.shape[1]), F32), unroll=8)
            width = w2_ref.shape[2]
            o_ref[:, kvg * width:(kvg + 1) * width] = _dot(_gelu_tanh(acc + pew).astype(BF16), w2_ref[kv])


def _pack_compress_pages(pe, w1, w2):
    d = HEAD_DIM
    nblk = PAGE_SIZE // CMP_BLOCK
    hid = w1.shape[-1]
    eye = jnp.eye(nblk, dtype=w1.dtype)
    w1r = w1.reshape(2, CMP_BLOCK, d, hid)
    wexp = jnp.einsum('kjdc,nm->kdnjmc', w1r, eye).reshape(2, d // 2, 2 * PAGE_SIZE, nblk * hid)
    w2bd = jnp.einsum('kce,nm->kncme', w2, eye).reshape(2, nblk * hid, nblk * w2.shape[-1])
    pe_rows = jnp.pad(pe.reshape(2, 1, CMP_BLOCK * d), ((0, 0), (0, 7), (0, 0)))
    return wexp.astype(BF16), pe_rows, w1.astype(BF16), w2bd.astype(BF16)


def _compress_pages(page_table_flat, pool_slabs, wexp, pe_rows, w1b, w2bd, *, base, pg):
    n_pages = page_table_flat.shape[0]
    slab = pool_slabs.shape[1]
    width = w2bd.shape[2]
    full3 = lambda a: pl.BlockSpec(a.shape, lambda i, pt: (0, 0, 0))
    return pl.pallas_call(
        functools.partial(_cmp_pages_kernel, pg=pg, base=base),
        out_shape=jax.ShapeDtypeStruct((n_pages, 2 * NSA_KV_HEADS * width), F32),
        grid_spec=pltpu.PrefetchScalarGridSpec(
            num_scalar_prefetch=1,
            grid=(n_pages // pg,),
            in_specs=[pl.BlockSpec(memory_space=pl.ANY), pl.BlockSpec(memory_space=pl.ANY),
                      full3(pe_rows), full3(w1b), full3(w2bd)],
            out_specs=pl.BlockSpec((pg, 2 * NSA_KV_HEADS * width), lambda i, pt: (i, 0)),
            scratch_shapes=[pltpu.VMEM((2, pg * (slab + 8), pool_slabs.shape[2]), F32), pltpu.VMEM(wexp.shape, BF16),
                            pltpu.SemaphoreType.DMA((3,))]),
        compiler_params=_cparams(("arbitrary",)),
        name="compress_pages",
    )(page_table_flat, pool_slabs, wexp, pe_rows, w1b, w2bd)


def _heads_to_lanes(o, ts):
    return jnp.concatenate([o[r * ts:(r + 1) * ts] for r in range(NSA_REP)], axis=1)


def _nsa_sample_a_kernel(q_ref, kcp_ref, kcn_ref, wc_ref, wn_ref, ocmp_ref, owin_ref, idx_ref, wout_ref,
                         *, past, n_new_blk, tv):
    ts = q_ref.shape[0]
    q = q_ref[...].astype(F32)
    hp = kcp_ref.shape[0]
    bpp = PAGE_SIZE // CMP_BLOCK
    plane_w = bpp * HEAD_DIM
    nbp = hp * bpp
    nsb_past = nbp * CMP_BLOCK // SEL_BLOCK
    nsb = nsb_past + (n_new_blk + 1) // 2
    wb = wc_ref.shape[1]
    m_rows = NSA_REP * ts
    row = lax.broadcasted_iota(jnp.int32, (m_rows, 1), 0)
    tpos = past + lax.rem(row, ts)
    t8 = past + lax.broadcasted_iota(jnp.int32, (ts, 1), 0)
    k_new = kcn_ref[...]
    w_old = wc_ref[...]
    w_new = wn_ref[...]
    nn = k_new.shape[0]
    pagei = lax.broadcasted_iota(jnp.int32, (1, hp), 1)
    cn = lax.broadcasted_iota(jnp.int32, (1, nn), 1)
    mask_p = jnp.concatenate([(bpp * pagei + n + 1) * CMP_BLOCK - 1 <= tpos for n in range(bpp)], axis=1)
    mask_n = ((nbp + cn + 1) * CMP_BLOCK - 1 <= tpos) & (cn < n_new_blk)
    kp_old = past - wb + lax.broadcasted_iota(jnp.int32, (1, wb), 1)
    kp_new = past + lax.broadcasted_iota(jnp.int32, (1, w_new.shape[0]), 1)
    wmask_old = (kp_old <= tpos) & (kp_old > tpos - WINDOW) & (kp_old >= 0)
    wmask_new = (kp_new <= tpos) & (kp_new > tpos - WINDOW)
    lane = lax.broadcasted_iota(jnp.int32, (1, LANES), 1)
    j = jnp.concatenate([2 * pagei, 2 * pagei + 1, nsb_past + lane], axis=1)
    width = 2 * hp + LANES

    new_t = jnp.concatenate([w_new, jnp.zeros((LANES - w_new.shape[0], w_new.shape[1]), F32)], axis=0).T
    wout_ref[...] = jnp.concatenate([w_old[:, tv:], new_t[:, 0:tv]], axis=1)

    def joint_softmax(parts):
        mx = None
        for s, mk in parts:
            cur = jnp.max(jnp.where(mk, s, NEG), axis=-1, keepdims=True)
            mx = cur if mx is None else jnp.maximum(mx, cur)
        mx = jnp.where(mx > 0.5 * NEG, mx, 0.0)
        es = [jnp.where(mk, jnp.exp(jnp.where(mk, s, NEG) - mx), 0.0) for s, mk in parts]
        den = sum(jnp.sum(e, axis=-1, keepdims=True) for e in es)
        inv = 1.0 / jnp.where(den > 0.0, den, 1.0)
        return [e * inv for e in es]

    def fold_heads(p):
        out = p[0:ts]
        for r in range(1, NSA_REP):
            out = out + p[r * ts:(r + 1) * ts]
        return out

    for g in range(NSA_KV_HEADS):
        qg = _stack_heads(q, g).astype(BF16)
        kcol = slice(g * HEAD_DIM, (g + 1) * HEAD_DIM)
        vcol = slice((NSA_KV_HEADS + g) * HEAD_DIM, (NSA_KV_HEADS + g + 1) * HEAD_DIM)

        kplanes = [kcp_ref[:, g * plane_w + n * HEAD_DIM:g * plane_w + (n + 1) * HEAD_DIM].astype(BF16)
                   for n in range(bpp)]
        vplanes = [kcp_ref[:, (NSA_KV_HEADS + g) * plane_w + n * HEAD_DIM:
                           (NSA_KV_HEADS + g) * plane_w + (n + 1) * HEAD_DIM].astype(BF16) for n in range(bpp)]
        s_past = jnp.concatenate([_dot_nt(qg, kp) for kp in kplanes], axis=1)
        p_p, p_n = joint_softmax([(s_past, mask_p), (_dot_nt(qg, k_new[:, kcol].astype(BF16)), mask_n)])
        o_cmp = _dot(p_n.astype(BF16), k_new[:, vcol].astype(BF16))
        for n in range(bpp):
            o_cmp = o_cmp + _dot(p_p[:, n * hp:(n + 1) * hp].astype(BF16), vplanes[n])
        ocmp_ref[:, g * NSA_REP * HEAD_DIM:(g + 1) * NSA_REP * HEAD_DIM] = _heads_to_lanes(o_cmp, ts)

        pp = fold_heads(p_p)
        pn = fold_heads(p_n)
        imp_new = jnp.zeros((ts, LANES), F32)
        for c in range(n_new_blk):
            imp_new = imp_new + jnp.where(lane == c // 2, pn[:, c:c + 1], 0.0)
        imp = jnp.concatenate([pp[:, 0:hp] + pp[:, hp:2 * hp], pp[:, 2 * hp:3 * hp] + pp[:, 3 * hp:4 * hp],
                               imp_new], axis=1)
        cur = t8 // SEL_BLOCK
        forced = (j == 0) | (j == cur) | (j == cur - 1)
        valid = j * SEL_BLOCK <= t8
        score = jnp.where(forced, -NEG, jnp.where(valid, imp, NEG))
        alive = j < nsb
        chosen = jnp.zeros((ts, LANES), jnp.int32)
        for n in range(min(N_SELECT, nsb)):
            best = jnp.max(jnp.where(alive, score, 2.0 * NEG), axis=-1, keepdims=True)
            pick = jnp.min(jnp.where(alive & (score == best), j, nsb), axis=-1, keepdims=True)
            alive = alive & (j != pick)
            chosen = jnp.where(lane == n, pick, chosen)
        idx_ref[g] = chosen

        kt_old = w_old[g * HEAD_DIM:(g + 1) * HEAD_DIM, :].astype(BF16)
        vt_old = w_old[(NSA_KV_HEADS + g) * HEAD_DIM:(NSA_KV_HEADS + g + 1) * HEAD_DIM, :].astype(BF16)
        pw_old, pw_new = joint_softmax([(_dot(qg, kt_old), wmask_old),
                                        (_dot_nt(qg, w_new[:, kcol].astype(BF16)), wmask_new)])
        o_win = _dot_nt(pw_old.astype(BF16), vt_old) + _dot(pw_new.astype(BF16), w_new[:, vcol].astype(BF16))
        owin_ref[:, g * NSA_REP * HEAD_DIM:(g + 1) * NSA_REP * HEAD_DIM] = _heads_to_lanes(o_win, ts)


def _nsa_sample_a(q, kc_past, kc_new, win_old, win_new, *, b, past, n_new_blk, tv):
    hp = kc_past.shape[0] // b
    blk3 = lambda a: pl.BlockSpec((None,) + a.shape[1:], lambda bb: (bb, 0, 0))
    return pl.pallas_call(
        functools.partial(_nsa_sample_a_kernel, past=past, n_new_blk=n_new_blk, tv=tv),
        out_shape=(jax.ShapeDtypeStruct((b, TS, 512), F32), jax.ShapeDtypeStruct((b, TS, 512), F32),
                   jax.ShapeDtypeStruct((b, NSA_KV_HEADS, TS, LANES), jnp.int32),
                   jax.ShapeDtypeStruct(win_old.shape, F32)),
        grid=(b,),
        in_specs=[blk3(q), pl.BlockSpec((hp, kc_past.shape[1]), lambda bb: (bb, 0)), blk3(kc_new), blk3(win_old),
                  blk3(win_new)],
        out_specs=(pl.BlockSpec((None, TS, 512), lambda bb: (bb, 0, 0)),
                   pl.BlockSpec((None, TS, 512), lambda bb: (bb, 0, 0)),
                   pl.BlockSpec((None, NSA_KV_HEADS, TS, LANES), lambda bb: (bb, 0, 0, 0)),
                   blk3(win_old)),
        compiler_params=_cparams(("parallel",)),
        name="nsa_sample_a",
    )(q, kc_past, kc_new, win_old, win_new)


def _nsa_sel_kernel(idx_ref, pt_ref, pool_ref, new_ref, q_ref, ocmp_ref, owin_ref, gate_ref, o_ref,
                    buf_ref, sem_ref, *, past, base, n_pages, tv, nsel):
    bb = pl.program_id(0)
    nb = pl.num_programs(0)
    ts = q_ref.shape[0]
    page = pool_ref.shape[2]
    blocks_per_page = page // SEL_BLOCK
    nsb_past = past // SEL_BLOCK
    kv_rows = NSA_KV_HEADS * HEAD_DIM

    def block_id(b_, g, t, n):
        return idx_ref[((b_ * NSA_KV_HEADS + g) * ts + t) * nsel + n]

    def copies(src_slab, slot, g, t, n):
        lanes = pl.ds(pl.multiple_of(n * page, page), page)
        return [pltpu.make_async_copy(src_slab.at[pl.ds(kv * kv_rows + g * HEAD_DIM, HEAD_DIM), :],
                                      buf_ref.at[slot, kv, g * tv + t, :, lanes], sem_ref.at[slot])
                for kv in range(2)]

    def fetch(b_, slot):
        for g in range(NSA_KV_HEADS):
            for t in range(tv):
                def body(n, c):
                    bid = block_id(b_, g, t, n)

                    @pl.when(bid < nsb_past)
                    def _():
                        src = pool_ref.at[base + pt_ref[b_ * n_pages + bid // blocks_per_page]]
                        for cp in copies(src, slot, g, t, n):
                            cp.start()

                    @pl.when(bid >= nsb_past)
                    def _():
                        for cp in copies(new_ref.at[b_], slot, g, t, n):
                            cp.start()
                    return c
                lax.fori_loop(0, nsel, body, 0)

    @pl.when(bb == 0)
    def _():
        fetch(0, 0)

    @pl.when(bb + 1 < nb)
    def _():
        fetch(bb + 1, lax.rem(bb + 1, 2))

    slot = lax.rem(bb, 2)

    def wait_body(n, c):
        for cp in copies(pool_ref.at[0], slot, 0, 0, 0):
            cp.wait()
        return c
    lax.fori_loop(0, NSA_KV_HEADS * tv * nsel, wait_body, 0)

    q = q_ref[...].astype(F32)
    gates = gate_ref[...]
    o_cmp = ocmp_ref[...]
    o_win = owin_ref[...]
    m_rows = NSA_REP * ts
    row = lax.broadcasted_iota(jnp.int32, (m_rows, 1), 0)
    trow = lax.rem(row, ts)
    tpos = past + trow
    nk = nsel * page
    lane = lax.broadcasted_iota(jnp.int32, (1, nk), 1)
    in_page = lax.rem(lane, page)
    outs = []
    for g in range(NSA_KV_HEADS):
        qg = _stack_heads(q, g).astype(BF16)
        o_sel = jnp.zeros((m_rows, HEAD_DIM), F32)
        for t in range(tv):
            kt = buf_ref[slot, 0, g * tv + t].astype(BF16)
            vt = buf_ref[slot, 1, g * tv + t].astype(BF16)
            first_pos = jnp.zeros((1, nk), jnp.int32)
            half = jnp.zeros((1, nk), jnp.int32)
            for n in range(nsel):
                bid = block_id(bb, g, t, n)
                here = lane // page == n
                first_pos = jnp.where(here, (bid // blocks_per_page) * page, first_pos)
                half = jnp.where(here, lax.rem(bid, blocks_per_page), half)
            kpos = first_pos + in_page
            mask = (in_page // SEL_BLOCK == half) & (kpos <= tpos)
            s = jnp.where(mask, _dot(qg, kt), NEG)
            e = jnp.where(mask, jnp.exp(s - jnp.max(s, axis=-1, keepdims=True)), 0.0)
            p = e / jnp.sum(e, axis=-1, keepdims=True)
            o_sel = o_sel + jnp.where(trow == t, _dot_nt(p.astype(BF16), vt), 0.0)
        for r in range(NSA_REP):
            h = g * NSA_REP + r
            hs = slice(h * HEAD_DIM, (h + 1) * HEAD_DIM)
            outs.append(o_cmp[:, hs] * gates[:, 3 * h:3 * h + 1] + o_sel[r * ts:(r + 1) * ts] * gates[:, 3 * h + 1:3 * h + 2]
                        + o_win[:, hs] * gates[:, 3 * h + 2:3 * h + 3])
    o_ref[...] = jnp.concatenate(outs, axis=1)


def _nsa_sample_sel(idx_flat, page_table_flat, pool_slabs, new_slabs, q, o_cmp, o_win, gates, *, b, past, base, tv):
    nsel = idx_flat.shape[0] // (b * NSA_KV_HEADS * TS)
    page = pool_slabs.shape[2]
    blk3 = lambda a: pl.BlockSpec((None,) + a.shape[1:], lambda bb, i_, p_: (bb, 0, 0))
    return pl.pallas_call(
        functools.partial(_nsa_sel_kernel, past=past, base=base, n_pages=page_table_flat.shape[0] // b,
                          tv=tv, nsel=nsel),
        out_shape=jax.ShapeDtypeStruct((b, TS, 512), F32),
        grid_spec=pltpu.PrefetchScalarGridSpec(
            num_scalar_prefetch=2,
            grid=(b,),
            in_specs=[pl.BlockSpec(memory_space=pl.ANY), pl.BlockSpec(memory_space=pl.ANY),
                      blk3(q), blk3(o_cmp), blk3(o_win), blk3(gates)],
            out_specs=pl.BlockSpec((None, TS, 512), lambda bb, i_, p_: (bb, 0, 0)),
            scratch_shapes=[pltpu.VMEM((2, 2, NSA_KV_HEADS * tv, HEAD_DIM, nsel * page), F32),
                            pltpu.SemaphoreType.DMA((2,))]),
        compiler_params=_cparams(("arbitrary",)),
        name="nsa_sample_sel",
    )(idx_flat, page_table_flat, pool_slabs, new_slabs, q, o_cmp, o_win, gates)


def _layer_weights(l, norm_mix, w_in, gla_wa2, gla_ba, gla_norm, cmp_pe, cmp_w1, cmp_w2, w_out, norm_mem,
                   w_mq, w_mkv, w_mo, norm_ffn, w_ff1, w_ff2):
    pe_big, w1_big, w2_big = _pack_compress(cmp_pe[l], cmp_w1[l], cmp_w2[l])
    return dict(
        pages=_pack_compress_pages(cmp_pe[l], cmp_w1[l], cmp_w2[l]), norm_mix=norm_mix[l][None, :], w_in=_pack_w_in(w_in[l]), wa2=_pack_wa2(gla_wa2[l]),
        ba=gla_ba[l][None, :], gnorm=gla_norm[l][None, :], pe_big=pe_big, w1_big=w1_big, w2_big=w2_big,
        w_out=w_out[l].astype(BF16), norm_mem=norm_mem[l][None, :], w_mq=w_mq[l].astype(BF16),
        w_mkv=w_mkv[l].astype(BF16), w_mo=w_mo[l].astype(BF16), norm_ffn=norm_ffn[l][None, :],
        w_ff1=w_ff1[l].astype(BF16), w_ff2=w_ff2[l].astype(BF16))


def _even_odd(kcv, b):
    nb = kcv.shape[0] // b
    return kcv.reshape(b, nb // 2, 2, -1).transpose(0, 2, 1, 3).reshape(b * nb, -1)


def _prompt_layer(x, mem, p, tabs, gfinal, *, b, t, final_norm):
    q, cmp_rows, slc_rows, win_rows, gates, gla, ret = _project(
        x, p['norm_mix'], p['w_in'], p['wa2'], p['ba'], tabs, tm=256)
    kcv = _compress(cmp_rows.reshape(-1, CMP_BLOCK * KV_W), p['pe_big'], p['w1_big'], p['w2_big'], tr=128)
    o_nsa = _nsa_prompt(q, _even_odd(kcv, b), slc_rows, win_rows, gates, b=b, t=t, tq=128, tk=256)
    sg0 = jnp.zeros((b, GLA_HEADS, GLA_DK, GLA_DV), F32)
    sr0 = jnp.zeros((b, RET_HEADS, RET_DIM, RET_DIM), F32)
    o_rec, s_gla, s_ret = _recurrent(gla, ret, p['gnorm'], sg0, sr0, b=b, t=t, tt=128, nbk=2 if b % 2 == 0 else 1)
    x = _outproj(x, o_nsa, o_rec, p['w_out'], tm=512)
    mem_kv = _matmul(mem, p['w_mkv'], tm=256)
    x = _cross(x, p['norm_mem'], p['w_mq'], p['w_mo'], mem_kv, b=b, t=t, tq=512)
    x = _ffn(x, p['norm_ffn'], p['w_ff1'], p['w_ff2'], gfinal, tm=1024, tf=1024, final_norm=final_norm)
    wlen = min(WINDOW, t)
    win_tail = win_rows.reshape(b, t, KV_W)[:, t - wlen:]
    return x, (cmp_rows, slc_rows, win_tail, s_gla, s_ret, mem_kv)


def _sample_layer(x, p, tabs, gfinal, cmp_slabs, slc_slabs, win_slabs, sg0, sr0, mem_cache, pt_flat,
                  *, b, past, tv, base, layer, final_norm):
    n = b * TS
    q, cmp_rows, slc_rows, win_rows, gates, gla, ret = _project(
        x, p['norm_mix'], p['w_in'], p['wa2'], p['ba'], tabs, tm=n, valid_period=(TS, tv))
    rows3 = lambda a: a.reshape(b, TS, a.shape[-1])
    pad_blk = lambda a: jnp.pad(rows3(a), ((0, 0), (0, SEL_BLOCK - TS), (0, 0)))
    n_new_blk = SEL_BLOCK // CMP_BLOCK
    pg = min(64, past // PAGE_SIZE)
    kc_past = _compress_pages(pt_flat, cmp_slabs, *p['pages'], base=base, pg=pg)
    kc_new = _compress(pad_blk(cmp_rows).reshape(b * n_new_blk, CMP_BLOCK * KV_W), p['pe_big'], p['w1_big'],
                       p['w2_big'], tr=b * n_new_blk)
    kc_new = jnp.pad(kc_new.reshape(b, n_new_blk, KV_W), ((0, 0), (0, TS - n_new_blk), (0, 0)))
    q3 = rows3(q)
    o_cmp, o_win, idx, new_win = _nsa_sample_a(q3, kc_past, kc_new, win_slabs, rows3(win_rows), b=b, past=past,
                                               n_new_blk=n_new_blk, tv=tv)
    nsel = min(N_SELECT, past // SEL_BLOCK + 1)
    slc_new = jnp.pad(jnp.swapaxes(rows3(slc_rows), 1, 2), ((0, 0), (0, 0), (0, cmp_slabs.shape[2] - TS)))
    o_nsa = _nsa_sample_sel(idx[..., :nsel].reshape(-1), pt_flat, slc_slabs, slc_new, q3, o_cmp, o_win,
                            rows3(gates), b=b, past=past, base=base, tv=tv)
    o_rec, s_gla, s_ret = _recurrent(gla, ret, p['gnorm'], sg0, sr0, b=b, t=TS, tt=TS, nbk=2 if b % 2 == 0 else 1, n_valid=tv)
    x = _outproj(x, o_nsa.reshape(n, -1), o_rec, p['w_out'], tm=n)
    x = _cross(x, p['norm_mem'], p['w_mq'], p['w_mo'], mem_cache, b=b, t=TS, tq=TS, layer=layer)
    x = _ffn(x, p['norm_ffn'], p['w_ff1'], p['w_ff2'], gfinal, tm=n, tf=1024, final_norm=final_norm)
    return x, (rows3(cmp_rows)[:, :tv], rows3(slc_rows)[:, :tv], new_win, s_gla, s_ret)


def kernel(x_prompt, x_sample, mem_prompt, cache_cmp_kv, cache_slc_kv, cache_win_kv, state_gla, state_ret,
           cache_mem_kv, page_table, norm_mix, w_in, gla_wa2, gla_ba, gla_norm, cmp_pe, cmp_w1, cmp_w2, w_out,
           norm_mem, w_mq, w_mkv, w_mo, norm_ffn, w_ff1, w_ff2, norm_final):
    depth = w_in.shape[0]
    b, t, d = x_prompt.shape
    g, hd = NSA_KV_HEADS, HEAD_DIM
    gfinal = norm_final[None, :]
    tabs_p = _all_rope_tables(jnp.arange(t))
    hp = x_prompt.reshape(b * t, d)
    mem = mem_prompt.reshape(-1, d)
    sb, tv = x_sample.shape[:2]
    assert tv <= TS
    past = page_table.shape[1] * PAGE_SIZE
    tabs_s = _all_rope_tables(past + jnp.arange(sb * TS) % TS)
    hs = jnp.pad(x_sample, ((0, 0), (0, TS - tv), (0, 0))).reshape(sb * TS, d)
    pt_flat = page_table.reshape(-1)
    n_pool = cache_cmp_kv.shape[1]
    cmp_slabs = _pool_slabs(cache_cmp_kv)
    slc_slabs = _pool_slabs(cache_slc_kv)
    win_slabs = jnp.transpose(cache_win_kv, (0, 1, 3, 4, 5, 2)).reshape(depth, sb, KV_W, -1)
    new_p = [[] for _ in range(6)]
    new_s = [[] for _ in range(5)]
    for l in range(depth):
        p = _layer_weights(l, norm_mix, w_in, gla_wa2, gla_ba, gla_norm, cmp_pe, cmp_w1, cmp_w2, w_out, norm_mem,
                           w_mq, w_mkv, w_mo, norm_ffn, w_ff1, w_ff2)
        last = l == depth - 1
        hp, st = _prompt_layer(hp, mem, p, tabs_p, gfinal, b=b, t=t, final_norm=last)
        for lst, a in zip(new_p, st):
            lst.append(a)
        hs, st = _sample_layer(hs, p, tabs_s, gfinal, cmp_slabs, slc_slabs, win_slabs[l], state_gla[l], state_ret[l],
                               cache_mem_kv, pt_flat, b=sb, past=past, tv=tv, base=l * n_pool, layer=l,
                               final_norm=last)
        for lst, a in zip(new_s, st):
            lst.append(a)
    cmp_p, slc_p, win_p, gla_p, ret_p, mem_p = [jnp.stack(a) for a in new_p]
    cmp_s, slc_s, win_s, gla_s, ret_s = [jnp.stack(a) for a in new_s]
    n_mem = mem_prompt.shape[1]
    kv6 = lambda a: a.reshape(a.shape[:3] + (2, g, hd))
    win_s = jnp.transpose(win_s.reshape(depth, sb, 2, g, hd, -1), (0, 1, 5, 2, 3, 4))
    return (hp.reshape(b, t, d), hs.reshape(sb, TS, d)[:, :tv],
            kv6(cmp_p.reshape(depth, b, t, KV_W)), kv6(slc_p.reshape(depth, b, t, KV_W)), kv6(win_p), gla_p, ret_p,
            mem_p.reshape(depth, b, n_mem, 2, MEM_HEADS, MEM_HD),
            kv6(cmp_s), kv6(slc_s), win_s, gla_s, ret_s)
```

```python
import functools
import math

import numpy as np
import jax
import jax.numpy as jnp
from jax import lax
from jax.experimental import pallas as pl
from jax.experimental.pallas import tpu as pltpu

F32 = jnp.float32
BF16 = jnp.bfloat16

D_MODEL = 1024
PAGE_SIZE = 128
HEAD_DIM = 64
NSA_HEADS = 8
NSA_KV_HEADS = 2
NSA_REP = NSA_HEADS // NSA_KV_HEADS
CMP_BLOCK = 32
SEL_BLOCK = 64
N_SELECT = 16
WINDOW = 512
ROT_DIM = HEAD_DIM // 4
ROPE_THETA = 500000.0
GLA_HEADS = 4
GLA_DK = 32
GLA_DV = 64
GLA_RANK = 16
GLA_TAU = 16.0
RET_HEADS = 4
RET_DIM = 64
RET_THETA = 10000.0
MEM_HEADS = 4
MEM_HD = D_MODEL // MEM_HEADS
D_FF = 4 * D_MODEL
EPS = 1e-6
IN_SIZES = (NSA_HEADS * HEAD_DIM, 6 * NSA_KV_HEADS * HEAD_DIM, 3 * NSA_HEADS,
            GLA_HEADS * GLA_DK, GLA_HEADS * GLA_DK, GLA_HEADS * GLA_DV, GLA_RANK, GLA_HEADS * GLA_DV,
            RET_HEADS * RET_DIM, RET_HEADS * RET_DIM, RET_HEADS * RET_DIM, RET_HEADS * RET_DIM)

LANES = 128
VMEM_LIMIT = 56 << 20
NEG = -1e30
M_INIT = -1e29
KV_W = 2 * NSA_KV_HEADS * HEAD_DIM

C_Q = 0
C_KV = 512
C_GLA = 1280
C_RET = 2048
C_TAIL = 3072
W_IN_COLS = 3200
GLA_W = 896
RET_W = 1024


def _cparams(sem, vmem=VMEM_LIMIT):
    return pltpu.CompilerParams(dimension_semantics=sem, vmem_limit_bytes=vmem)


def _dot(a, b):
    return jnp.dot(a, b, preferred_element_type=F32)


def _dot_nt(a, b):
    return lax.dot_general(a, b, (((1,), (1,)), ((), ())), preferred_element_type=F32)


def _dot_tn(a, b):
    return lax.dot_general(a, b, (((0,), (0,)), ((), ())), preferred_element_type=F32)


def _split3(x):
    hi = x.astype(BF16)
    r = x - hi.astype(F32)
    mid = r.astype(BF16)
    lo = (r - mid.astype(F32)).astype(BF16)
    return hi, mid, lo


def _rms(x, eps=EPS):
    return x * lax.rsqrt(jnp.mean(x * x, axis=-1, keepdims=True) + eps)


def _silu(x):
    return x * jax.nn.sigmoid(x)


def _rope128(v, c, sa, sb, half):
    return v * c + pltpu.roll(v, LANES - half, 1) * sa + pltpu.roll(v, half, 1) * sb


def _proj_kernel(x_ref, gain_ref, w_ref, wa2_ref, ba_ref, nc_ref, nsa_ref, nsb_ref, rc_ref, rsa_ref, rsb_ref,
                 q_ref, cmp_ref, slc_ref, win_ref, gate_ref, gla_ref, ret_ref, *, valid_period):
    x = x_ref[...]
    xn = _rms(x) * gain_ref[...]
    if valid_period is not None:
        period, n_valid = valid_period
        row = lax.broadcasted_iota(jnp.int32, (x.shape[0], 1), 0)
        xn = jnp.where(lax.rem(row, period) < n_valid, xn, 0.0)
    xb = xn.astype(BF16)

    def mm(a, b):
        return _dot(xb, w_ref[:, a:b])

    nc, nsa, nsb = nc_ref[...], nsa_ref[...], nsb_ref[...]
    rc, rsa, rsb = rc_ref[...], rsa_ref[...], rsb_ref[...]
    half_n = ROT_DIM // 2
    half_r = RET_DIM // 2

    q = mm(C_Q, C_KV)
    for c in range(4):
        sl = slice(c * LANES, (c + 1) * LANES)
        q_ref[:, sl] = (_rope128(q[:, sl], nc, nsa, nsb, half_n) * (HEAD_DIM ** -0.5)).astype(q_ref.dtype)

    kv = mm(C_KV, C_GLA)
    for br, ref in enumerate((cmp_ref, slc_ref, win_ref)):
        ref[:, 0:LANES] = _rope128(kv[:, br * KV_W:br * KV_W + LANES], nc, nsa, nsb, half_n)
        ref[:, LANES:KV_W] = kv[:, br * KV_W + LANES:(br + 1) * KV_W]

    tail = mm(C_TAIL, W_IN_COLS)
    gate_ref[...] = jax.nn.sigmoid(tail)
    z = jnp.dot(tail, wa2_ref[...], preferred_element_type=F32, precision=lax.Precision.HIGHEST) + ba_ref[...]
    log_a = (jnp.minimum(z, 0.0) - jnp.log1p(jnp.exp(-jnp.abs(z)))) * (1.0 / GLA_TAU)

    gl = mm(C_GLA, C_RET)
    gla_ref[:, 0:128] = gl[:, 0:128] * (GLA_DK ** -0.5)
    gla_ref[:, 128:256] = gl[:, 128:256]
    gla_ref[:, 256:384] = log_a
    gla_ref[:, 384:640] = gl[:, 256:512]
    gla_ref[:, 640:896] = _silu(gl[:, 512:768])

    rt = mm(C_RET, C_TAIL)
    for c in range(2):
        sl = slice(c * LANES, (c + 1) * LANES)
        ret_ref[:, sl] = _rope128(rt[:, sl], rc, rsa, rsb, half_r)
        sk = slice(256 + c * LANES, 256 + (c + 1) * LANES)
        ret_ref[:, sk] = _rope128(rt[:, sk], rc, rsa, rsb, half_r) * (RET_DIM ** -0.5)
    ret_ref[:, 512:768] = rt[:, 512:768]
    ret_ref[:, 768:1024] = _silu(rt[:, 768:1024])


def _project(x, gain, w, wa2, ba, tabs, *, tm, valid_period=None):
    n = x.shape[0]
    p = tabs[0].shape[0]
    nt = p // tm
    row = lambda w_: pl.BlockSpec((tm, w_), lambda i: (i, 0))
    full = lambda a: pl.BlockSpec(a.shape, lambda i: (0, 0))
    tab = pl.BlockSpec((tm, LANES), lambda i: (i % nt, 0))
    out_shape = (jax.ShapeDtypeStruct((n, 512), BF16),
                 jax.ShapeDtypeStruct((n, KV_W), F32), jax.ShapeDtypeStruct((n, KV_W), F32),
                 jax.ShapeDtypeStruct((n, KV_W), F32), jax.ShapeDtypeStruct((n, LANES), F32),
                 jax.ShapeDtypeStruct((n, GLA_W), F32), jax.ShapeDtypeStruct((n, RET_W), F32))
    return pl.pallas_call(
        functools.partial(_proj_kernel, valid_period=valid_period),
        out_shape=out_shape,
        grid=(n // tm,),
        in_specs=[row(D_MODEL), full(gain), full(w), full(wa2), full(ba)] + [tab] * 6,
        out_specs=(row(512), row(KV_W), row(KV_W), row(KV_W), row(LANES), row(GLA_W), row(RET_W)),
        compiler_params=_cparams(("parallel",)),
        name="proj",
    )(x, gain, w, wa2, ba, *tabs)


def _rope_tables(pos, inv_freq):
    nf = inv_freq.shape[0]
    ang = pos.astype(F32)[:, None] * inv_freq[None, :]
    cos, sin = jnp.cos(ang), jnp.sin(ang)
    ones = jnp.ones((pos.shape[0], HEAD_DIM - 2 * nf), F32)
    zeros = jnp.zeros((pos.shape[0], HEAD_DIM - nf), F32)
    c = jnp.concatenate([cos, cos, ones], axis=1)
    sa = jnp.concatenate([-sin, zeros], axis=1)
    sb = jnp.concatenate([zeros[:, :nf], sin, zeros[:, :HEAD_DIM - 2 * nf]], axis=1)
    return tuple(jnp.tile(t, (1, LANES // HEAD_DIM)) for t in (c, sa, sb))


def _all_rope_tables(pos):
    nsa_f = ROPE_THETA ** (-jnp.arange(0, ROT_DIM, 2, dtype=F32) / ROT_DIM)
    ret_f = RET_THETA ** (-jnp.linspace(0.0, 1.0, RET_DIM // 2, dtype=F32))
    return _rope_tables(pos, nsa_f) + _rope_tables(pos, ret_f)


def _pack_w_in(w):
    offs = np.cumsum((0,) + IN_SIZES)
    nq, nkv, ngate, gq, gk, gv, ga, gr, rq, rk, rv, rg = [w[:, offs[i]:offs[i + 1]] for i in range(12)]
    pad = jnp.zeros((w.shape[0], W_IN_COLS - C_TAIL - ngate.shape[1] - ga.shape[1]), w.dtype)
    return jnp.concatenate([nq, nkv, gq, gk, gv, gr, rq, rk, rv, rg, ngate, ga, pad], axis=1).astype(BF16)


def _pack_wa2(wa2):
    top = jnp.zeros((3 * NSA_HEADS, wa2.shape[1]), wa2.dtype)
    bot = jnp.zeros((LANES - 3 * NSA_HEADS - GLA_RANK, wa2.shape[1]), wa2.dtype)
    return jnp.concatenate([top, wa2, bot], axis=0)


def _gelu_tanh(x):
    return 0.5 * x * (1.0 + jnp.tanh(0.7978845608028654 * (x + 0.044715 * (x * x * x))))


def _compress_kernel(x_ref, pe_ref, w1_ref, w2_ref, o_ref):
    xb = (x_ref[...] + pe_ref[...]).astype(BF16)
    h = _gelu_tanh(_dot(xb, w1_ref[...]))
    o_ref[...] = _dot(h.astype(BF16), w2_ref[...])


def _compress(rows_flat, pe_big, w1_big, w2_big, *, tr):
    r, k = rows_flat.shape
    tr = min(tr, r)
    full = lambda a: pl.BlockSpec(a.shape, lambda i: (0, 0))
    return pl.pallas_call(
        _compress_kernel,
        out_shape=jax.ShapeDtypeStruct((r, KV_W), F32),
        grid=(r // tr,),
        in_specs=[pl.BlockSpec((tr, k), lambda i: (i, 0)), full(pe_big), full(w1_big), full(w2_big)],
        out_specs=pl.BlockSpec((tr, KV_W), lambda i: (i, 0)),
        compiler_params=_cparams(("parallel",)),
        name="compress",
    )(rows_flat, pe_big, w1_big, w2_big)


def _pack_compress(pe, w1, w2):
    g, d = NSA_KV_HEADS, HEAD_DIM
    hid = w1.shape[-1]
    pe_big = jnp.broadcast_to(pe.transpose(1, 0, 2)[:, :, None, :], (CMP_BLOCK, 2, g, d)).reshape(1, -1)
    eye = jnp.eye(2 * g, dtype=w1.dtype).reshape(2, g, 2, g)
    w1r = w1.reshape(2, CMP_BLOCK, d, hid)
    w1_big = jnp.einsum('kjdc,kgKG->jkgdKGc', w1r, eye).reshape(CMP_BLOCK * KV_W, 2 * g * hid)
    w2_big = jnp.einsum('kce,kgKG->kgcKGe', w2, eye).reshape(2 * g * hid, KV_W)
    return pe_big, w1_big.astype(BF16), w2_big.astype(BF16)


def _stack_heads(q, g):
    return jnp.concatenate([q[:, (g * NSA_REP + r) * HEAD_DIM:(g * NSA_REP + r + 1) * HEAD_DIM]
                            for r in range(NSA_REP)], axis=0)


def _flash_init(n_chains, nq):
    return tuple((jnp.full((1, nq), M_INIT, F32), jnp.zeros((1, nq), F32), jnp.zeros((HEAD_DIM, nq), F32))
                 for _ in range(n_chains))


def _flash_finish(carry):
    return [acc / l_fin for _, l_fin, acc in carry]


def _flash_t(chains, tpos, lo, hi, tk, carry):
    def body(kt, carry):
        k0 = pl.multiple_of(kt * tk, tk)
        kpos = k0 + lax.broadcasted_iota(jnp.int32, (tk, 1), 0)
        out = []
        for (kv_ref, kcol, vcol, q_t, mask_fn), (m_old, l_old, acc) in zip(chains, carry):
            kb = kv_ref[pl.ds(k0, tk), kcol].astype(BF16)
            vb = kv_ref[pl.ds(k0, tk), vcol].astype(BF16)
            s_t = mask_fn(_dot(kb, q_t), k0, kpos, tpos)
            m_new = jnp.maximum(m_old, jnp.max(s_t, axis=0, keepdims=True))
            alpha = jnp.exp(m_old - m_new)
            p = jnp.exp(s_t - m_new)
            l_new = alpha * l_old + jnp.sum(p, axis=0, keepdims=True)
            out.append((m_new, l_new, alpha * acc + _dot_tn(vb, p.astype(BF16))))
        return tuple(out)

    return lax.fori_loop(lo, hi, body, carry)


def _nsa_prompt_kernel(q_ref, kcv_ref, slc_ref, win_ref, gate_ref, exp_ref, o_ref, selx_ref, *, tq, tk):
    i = pl.program_id(1)
    mq = NSA_REP * tq
    nb = kcv_ref.shape[0]
    nsb = nb // 2
    q_t = q_ref[...].astype(F32).T
    gate_t = gate_ref[...].T
    tcol = i * tq + lax.rem(lax.broadcasted_iota(jnp.int32, (1, mq), 1), tq)
    tq_col = i * tq + lax.broadcasted_iota(jnp.int32, (1, tq), 1)
    blk_row = lax.broadcasted_iota(jnp.int32, (nb, 1), 0)
    blk = jnp.where(blk_row < nsb, 2 * blk_row, 2 * (blk_row - nsb) + 1)
    cmask = (blk + 1) * CMP_BLOCK - 1 <= tcol
    jrow = lax.broadcasted_iota(jnp.int32, (nsb, 1), 0)
    cur = tq_col // SEL_BLOCK
    forced = (jrow == 0) | (jrow == cur) | (jrow == cur - 1)
    valid = jrow * SEL_BLOCK <= tq_col
    n_kt = ((i + 1) * tq + tk - 1) // tk
    win_lo = jnp.maximum(i * tq - WINDOW + 1, 0) // tk

    def sel_mask(g, causal):
        def fn(s, k0, kpos, tpos):
            bias = selx_ref[g, pl.ds(k0, tk), :]
            s = s + jnp.concatenate([bias] * NSA_REP, axis=1)
            return jnp.where(kpos <= tpos, s, NEG) if causal else s
        return fn

    def win_mask(s, k0, kpos, tpos):
        return jnp.where(kpos <= tpos, jnp.where(kpos > tpos - WINDOW, s, NEG), NEG)

    kcols = [slice(g * HEAD_DIM, (g + 1) * HEAD_DIM) for g in range(NSA_KV_HEADS)]
    vcols = [slice((NSA_KV_HEADS + g) * HEAD_DIM, (NSA_KV_HEADS + g + 1) * HEAD_DIM) for g in range(NSA_KV_HEADS)]
    qgs, o_cmps = [], []
    for g in range(NSA_KV_HEADS):
        qg = jnp.concatenate([q_t[(g * NSA_REP + r) * HEAD_DIM:(g * NSA_REP + r + 1) * HEAD_DIM, :]
                              for r in range(NSA_REP)], axis=1).astype(BF16)
        qgs.append(qg)
        kcol, vcol = kcols[g], vcols[g]

        s_t = _dot(kcv_ref[:, kcol].astype(BF16), qg)
        sm = jnp.where(cmask, s_t, NEG)
        mx = jnp.max(sm, axis=0, keepdims=True)
        mx = jnp.where(mx > 0.5 * NEG, mx, 0.0)
        e = jnp.where(cmask, jnp.exp(sm - mx), 0.0)
        den = jnp.sum(e, axis=0, keepdims=True)
        p = e / jnp.where(den > 0.0, den, 1.0)
        o_cmp = _dot_tn(kcv_ref[:, vcol].astype(BF16), p.astype(BF16))
        ps = p[:, 0:tq]
        for r in range(1, NSA_REP):
            ps = ps + p[:, r * tq:(r + 1) * tq]
        imp = ps[0:nsb] + ps[nsb:nb]

        score = jnp.where(forced, -NEG, jnp.where(valid, imp, NEG))
        rank = jnp.zeros((nsb, tq), F32)
        for c in range(nsb):
            sc = score[c:c + 1, :]
            beats = (sc > score) | ((sc == score) & (jrow > c))
            rank = rank + jnp.where(beats, 1.0, 0.0)
        sel = jnp.where(rank < float(N_SELECT), 1.0, 0.0).astype(BF16)
        selx_ref[g] = (_dot(exp_ref[...], sel) - 1.0) * (-NEG)
        o_cmps.append(o_cmp)

    groups = range(NSA_KV_HEADS)
    carry = _flash_t([(slc_ref, kcols[g], vcols[g], qgs[g], sel_mask(g, False)) for g in groups], tcol,
                     0, win_lo, tk, _flash_init(NSA_KV_HEADS, mq))
    carry = _flash_t([(slc_ref, kcols[g], vcols[g], qgs[g], sel_mask(g, True)) for g in groups]
                     + [(win_ref, kcols[g], vcols[g], qgs[g], win_mask) for g in groups], tcol,
                     win_lo, n_kt, tk, carry + _flash_init(NSA_KV_HEADS, mq))
    o_sels = _flash_finish(carry[:NSA_KV_HEADS])
    o_wins = _flash_finish(carry[NSA_KV_HEADS:])

    for g in groups:
        outs = []
        for r in range(NSA_REP):
            h = g * NSA_REP + r
            cs = slice(r * tq, (r + 1) * tq)
            outs.append(o_cmps[g][:, cs] * gate_t[3 * h:3 * h + 1, :]
                        + o_sels[g][:, cs] * gate_t[3 * h + 1:3 * h + 2, :]
                        + o_wins[g][:, cs] * gate_t[3 * h + 2:3 * h + 3, :])
        o_ref[:, g * NSA_REP * HEAD_DIM:(g + 1) * NSA_REP * HEAD_DIM] = jnp.concatenate(outs, axis=0).T.astype(o_ref.dtype)


def _nsa_prompt(q, kcv, slc, win, gates, *, b, t, tq, tk):
    n = b * t
    nq = t // tq
    nb = t // CMP_BLOCK
    nsb = t // SEL_BLOCK
    tk = min(tk, t)
    expand = (np.arange(t)[:, None] // SEL_BLOCK == np.arange(nsb)[None, :]).astype(np.float32)
    expand = jnp.asarray(expand, BF16)
    return pl.pallas_call(
        functools.partial(_nsa_prompt_kernel, tq=tq, tk=tk),
        out_shape=jax.ShapeDtypeStruct((n, NSA_HEADS * HEAD_DIM), BF16),
        grid=(b, nq),
        in_specs=[pl.BlockSpec((tq, 512), lambda bb, i: (bb * nq + i, 0)),
                  pl.BlockSpec((nb, KV_W), lambda bb, i: (bb, 0)),
                  pl.BlockSpec((t, KV_W), lambda bb, i: (bb, 0)),
                  pl.BlockSpec((t, KV_W), lambda bb, i: (bb, 0)),
                  pl.BlockSpec((tq, LANES), lambda bb, i: (bb * nq + i, 0)),
                  pl.BlockSpec((t, nsb), lambda bb, i: (0, 0))],
        out_specs=pl.BlockSpec((tq, 512), lambda bb, i: (bb * nq + i, 0)),
        scratch_shapes=[pltpu.VMEM((NSA_KV_HEADS, t, tq), F32)],
        compiler_params=_cparams(("parallel", "arbitrary")),
        name="nsa_prompt",
    )(q, kcv, slc, win, gates, expand)


def _gla_level_matrix(tt):
    lv = int(math.log2(tt))
    m = np.zeros((lv, tt, tt), np.float32)
    t = np.arange(tt)
    for l in range(lv):
        half = 1 << l
        split = ((t >> (l + 1)) << (l + 1)) + half - 1
        u = np.arange(tt)[None, :]
        upper = t > split
        m[l] = np.where(upper[:, None], (u > split[:, None]) & (u <= t[:, None]),
                        (u > t[:, None]) & (u <= split[:, None]))
    return m.reshape(lv * tt, tt)


def _ret_log_decay():
    return [float(np.log(np.float32(1.0) - np.float32(2.0) ** np.float32(-5.0 - h))) for h in range(RET_HEADS)]


def _recur_kernel(gla_ref, ret_ref, gn_ref, lvl_ref, tril_ref, sg0_ref, sr0_ref,
                  o_ref, sg_out_ref, sr_out_ref, sg_ref, sr_ref, *, tt, n_valid):
    ti = pl.program_id(1)
    nt = pl.num_programs(1)

    @pl.when(ti == 0)
    def _():
        sg_ref[...] = sg0_ref[...]
        sr_ref[...] = sr0_ref[...]

    for bi in range(gla_ref.shape[0]):
        _recur_tile(gla_ref.at[bi], ret_ref.at[bi], gn_ref, lvl_ref, tril_ref, o_ref.at[bi], sg_ref.at[bi],
                    sr_ref.at[bi], tt=tt, n_valid=n_valid)

    @pl.when(ti == nt - 1)
    def _():
        sg_out_ref[...] = sg_ref[...]
        sr_out_ref[...] = sr_ref[...]


def _recur_tile(gla_ref, ret_ref, gn_ref, lvl_ref, tril_ref, o_ref, sg_ref, sr_ref, *, tt, n_valid):
    levels = int(math.log2(tt))
    rowi = lax.broadcasted_iota(jnp.int32, (tt, 1), 0)
    coli = lax.broadcasted_iota(jnp.int32, (1, tt), 1)

    gq = gla_ref[:, 0:128]
    gk = gla_ref[:, 128:256]
    la = gla_ref[:, 256:384]
    gv = gla_ref[:, 384:640]
    gr = gla_ref[:, 640:896]
    if n_valid < tt:
        la = jnp.where(rowi < n_valid, la, 0.0)
    parts = _split3(la)
    tril = tril_ref[...]
    lvl = lvl_ref[...]
    ones = jnp.ones((tt, GLA_DV), BF16)
    cum = sum(_dot(tril, pt) for pt in parts)
    dlv = sum(_dot(lvl, pt) for pt in parts)
    tot = sum(_dot_tn(pt, ones) for pt in parts)
    q_dec = gq * jnp.exp(cum)
    k_dec = gk * jnp.exp(cum[tt - 1:tt, :] - cum)
    gla_out = []
    for h in range(GLA_HEADS):
        sl = slice(h * GLA_DK, (h + 1) * GLA_DK)
        vs = slice(h * GLA_DV, (h + 1) * GLA_DV)
        qh, kh = gq[:, sl], gk[:, sl]
        vh = gv[:, vs].astype(BF16)
        attn = jnp.where(rowi == coli, _dot_nt(qh.astype(BF16), kh.astype(BF16)), 0.0)
        for l in range(levels):
            ed = jnp.exp(dlv[l * tt:(l + 1) * tt, sl])
            upper = ((rowi >> l) & 1) == 1
            qe = jnp.where(upper, qh * ed, 0.0).astype(BF16)
            ke = jnp.where(upper, 0.0, kh * ed).astype(BF16)
            same = (rowi >> (l + 1)) == (coli >> (l + 1))
            attn = attn + jnp.where(same, _dot_nt(qe, ke), 0.0)
        s_h = sg_ref[h]
        o = _dot(attn.astype(BF16), vh) + _dot(q_dec[:, sl].astype(BF16), s_h.astype(BF16))
        sg_ref[h] = jnp.exp(tot[sl, :]) * s_h + _dot_tn(k_dec[:, sl].astype(BF16), vh)
        gla_out.append(_rms(o) * gn_ref[...] * gr[:, vs])
    o_ref[:, 0:256] = jnp.concatenate(gla_out, axis=1).astype(o_ref.dtype)

    rel = (rowi - coli).astype(F32)
    pos1 = (rowi + 1).astype(F32)
    left = (n_valid - 1 - rowi).astype(F32)
    ret_out = []
    for h, lg in enumerate(_ret_log_decay()):
        sl = slice(h * RET_DIM, (h + 1) * RET_DIM)
        qh = ret_ref[:, sl]
        kh = ret_ref[:, 256 + h * RET_DIM:256 + (h + 1) * RET_DIM]
        vh = ret_ref[:, 512 + h * RET_DIM:512 + (h + 1) * RET_DIM].astype(BF16)
        gh = ret_ref[:, 768 + h * RET_DIM:768 + (h + 1) * RET_DIM]
        dmat = jnp.where(rel >= 0.0, jnp.exp(lg * jnp.maximum(rel, 0.0)), 0.0)
        attn = _dot_nt(qh.astype(BF16), kh.astype(BF16)) * dmat
        s_h = sr_ref[h]
        o = _dot(attn.astype(BF16), vh) + _dot(qh.astype(BF16), s_h.astype(BF16)) * jnp.exp(lg * pos1)
        sr_ref[h] = math.exp(lg * n_valid) * s_h + _dot_tn((kh * jnp.exp(lg * left)).astype(BF16), vh)
        ret_out.append(_rms(o) * gh)
    o_ref[:, 256:512] = jnp.concatenate(ret_out, axis=1).astype(o_ref.dtype)


def _recurrent(gla, ret, gnorm, sg0, sr0, *, b, t, tt, nbk, n_valid=None):
    nt = t // tt
    n_valid = tt if n_valid is None else n_valid
    assert n_valid == tt or nt == 1
    lvl = jnp.asarray(_gla_level_matrix(tt), BF16)
    tril = jnp.asarray(np.tril(np.ones((tt, tt), np.float32)), BF16)
    full2 = lambda a: pl.BlockSpec(a.shape, lambda bb, i: (0, 0))
    st = lambda a: pl.BlockSpec((nbk,) + a.shape[1:], lambda bb, i: (bb, 0, 0, 0))
    tok = lambda w: pl.BlockSpec((nbk, tt, w), lambda bb, i: (bb, i, 0))
    out, s_gla, s_ret = pl.pallas_call(
        functools.partial(_recur_kernel, tt=tt, n_valid=n_valid),
        out_shape=(jax.ShapeDtypeStruct((b, t, 512), BF16 if tt % 16 == 0 else F32),
                   jax.ShapeDtypeStruct(sg0.shape, F32), jax.ShapeDtypeStruct(sr0.shape, F32)),
        grid=(b // nbk, nt),
        in_specs=[tok(GLA_W), tok(RET_W), full2(gnorm), full2(lvl), full2(tril), st(sg0), st(sr0)],
        out_specs=(tok(512), st(sg0), st(sr0)),
        scratch_shapes=[pltpu.VMEM((nbk,) + sg0.shape[1:], F32), pltpu.VMEM((nbk,) + sr0.shape[1:], F32)],
        compiler_params=_cparams(("parallel", "arbitrary")),
        name="recurrent",
    )(gla.reshape(b, t, GLA_W), ret.reshape(b, t, RET_W), gnorm, lvl, tril, sg0, sr0)
    return out.reshape(b * t, 512), s_gla, s_ret


def _outproj_kernel(x_ref, a_ref, b_ref, w_ref, o_ref):
    ka = a_ref.shape[1]
    o_ref[...] = (x_ref[...] + _dot(a_ref[...].astype(BF16), w_ref[0:ka, :])
                  + _dot(b_ref[...].astype(BF16), w_ref[ka:, :]))


def _outproj(x, a, b_, w, *, tm):
    n = x.shape[0]
    row = lambda a_: pl.BlockSpec((tm, a_.shape[1]), lambda i: (i, 0))
    return pl.pallas_call(
        _outproj_kernel,
        out_shape=jax.ShapeDtypeStruct(x.shape, F32),
        grid=(n // tm,),
        in_specs=[row(x), row(a), row(b_), pl.BlockSpec(w.shape, lambda i: (0, 0))],
        out_specs=row(x),
        compiler_params=_cparams(("parallel",)),
        name="outproj",
    )(x, a, b_, w)


def _cross_kernel(x_ref, gain_ref, wq_ref, wo_ref, kv_ref, o_ref):
    x = x_ref[...]
    xb = (_rms(x) * gain_ref[...]).astype(BF16)
    q = _dot(xb, wq_ref[...])
    heads = []
    for h in range(MEM_HEADS):
        qh = q[:, h * MEM_HD:(h + 1) * MEM_HD].astype(BF16)
        if len(kv_ref.shape) == 4:
            kh = kv_ref[:, 0, h, :].astype(BF16)
            vh = kv_ref[:, 1, h, :].astype(BF16)
        else:
            kh = kv_ref[:, h * MEM_HD:(h + 1) * MEM_HD].astype(BF16)
            vh = kv_ref[:, (MEM_HEADS + h) * MEM_HD:(MEM_HEADS + h + 1) * MEM_HD].astype(BF16)
        s = _dot_nt(qh, kh) * (MEM_HD ** -0.5)
        e = jnp.exp(s - jnp.max(s, axis=-1, keepdims=True))
        p = e / jnp.sum(e, axis=-1, keepdims=True)
        heads.append(_dot(p.astype(BF16), vh))
    att = jnp.concatenate(heads, axis=1).astype(BF16)
    o_ref[...] = x + _dot(att, wo_ref[...])


def _cross(x, gain, wq, wo, memkv, *, b, t, tq, layer=None):
    nq = t // tq
    full = lambda a: pl.BlockSpec(a.shape, lambda bb, i: (0, 0))
    row = pl.BlockSpec((tq, D_MODEL), lambda bb, i: (bb * nq + i, 0))
    if layer is None:
        kv_spec = pl.BlockSpec((memkv.shape[0] // b, memkv.shape[1]), lambda bb, i: (bb, 0))
    else:
        kv_spec = pl.BlockSpec((None, None) + memkv.shape[2:], lambda bb, i: (layer, bb, 0, 0, 0, 0))
    return pl.pallas_call(
        _cross_kernel,
        out_shape=jax.ShapeDtypeStruct(x.shape, F32),
        grid=(b, nq),
        in_specs=[row, full(gain), full(wq), full(wo), kv_spec],
        out_specs=row,
        compiler_params=_cparams(("parallel", "arbitrary")),
        name="cross",
    )(x, gain, wq, wo, memkv)


def _ffn_kernel(x_ref, gain_ref, w1_ref, w2_ref, gf_ref, o_ref, xn_ref, acc_ref, *, final_norm):
    j = pl.program_id(1)

    @pl.when(j == 0)
    def _():
        xn_ref[...] = (_rms(x_ref[...]) * gain_ref[...]).astype(BF16)
        acc_ref[...] = x_ref[...]

    h = jnp.maximum(_dot(xn_ref[...], w1_ref[...]), 0.0)
    acc_ref[...] += _dot((h * h).astype(BF16), w2_ref[...])

    @pl.when(j == pl.num_programs(1) - 1)
    def _():
        y = acc_ref[...]
        if final_norm:
            y = _rms(y) * gf_ref[...]
        o_ref[...] = y


def _ffn(x, gain, w1, w2, gfinal, *, tm, tf, final_norm):
    n = x.shape[0]
    row = pl.BlockSpec((tm, D_MODEL), lambda i, j: (i, 0))
    vec = pl.BlockSpec((1, D_MODEL), lambda i, j: (0, 0))
    return pl.pallas_call(
        functools.partial(_ffn_kernel, final_norm=final_norm),
        out_shape=jax.ShapeDtypeStruct(x.shape, F32),
        grid=(n // tm, D_FF // tf),
        in_specs=[row, vec, pl.BlockSpec((D_MODEL, tf), lambda i, j: (0, j)),
                  pl.BlockSpec((tf, D_MODEL), lambda i, j: (j, 0)), vec],
        out_specs=row,
        scratch_shapes=[pltpu.VMEM((tm, D_MODEL), BF16), pltpu.VMEM((tm, D_MODEL), F32)],
        compiler_params=_cparams(("parallel", "arbitrary")),
        name="ffn",
    )(x, gain, w1, w2, gfinal)


def _matmul_kernel(x_ref, w_ref, o_ref):
    o_ref[...] = _dot(x_ref[...].astype(BF16), w_ref[...])


def _matmul(x, w, *, tm):
    n, k = x.shape
    return pl.pallas_call(
        _matmul_kernel,
        out_shape=jax.ShapeDtypeStruct((n, w.shape[1]), F32),
        grid=(n // tm,),
        in_specs=[pl.BlockSpec((tm, k), lambda i: (i, 0)), pl.BlockSpec(w.shape, lambda i: (0, 0))],
        out_specs=pl.BlockSpec((tm, w.shape[1]), lambda i: (i, 0)),
        compiler_params=_cparams(("parallel",)),
        name="memkv",
    )(x, w)


TS = 8


def _pool_slabs(pool):
    depth, n_pool = pool.shape[:2]
    return jnp.transpose(pool, (0, 1, 3, 4, 5, 2)).reshape(depth * n_pool, KV_W, pool.shape[2])


def _cmp_pages_kernel(pt_ref, pool_ref, wexp_hbm, pe_ref, w1_ref, w2_ref, o_ref, buf_ref, wbuf_ref, sem_ref,
                      *, pg, base):
    s = pl.program_id(0)
    ns = pl.num_programs(0)
    slab = pool_ref.shape[1]
    pitch = buf_ref.shape[1] // pg
    n_pairs = wbuf_ref.shape[1]
    blocks_per_page = PAGE_SIZE // CMP_BLOCK

    def page_copy(page, slot, j):
        return pltpu.make_async_copy(pool_ref.at[page], buf_ref.at[slot, pl.ds(j * pitch, slab), :], sem_ref.at[slot])

    def fetch(step, slot):
        def body(j, c):
            page_copy(base + pt_ref[step * pg + j], slot, j).start()
            return c
        lax.fori_loop(0, pg, body, 0)

    @pl.when(s == 0)
    def _():
        weights = pltpu.make_async_copy(wexp_hbm, wbuf_ref, sem_ref.at[2])
        weights.start()
        fetch(0, 0)
        weights.wait()

    @pl.when(s + 1 < ns)
    def _():
        fetch(s + 1, lax.rem(s + 1, 2))

    slot = lax.rem(s, 2)

    def wait_body(j, c):
        page_copy(0, slot, j).wait()
        return c
    lax.fori_loop(0, pg, wait_body, 0)

    for kv in range(2):
        pew = sum(_dot(part, w1_ref[kv]) for part in _split3(pe_ref[kv]))[0:1]
        pew = jnp.concatenate([pew] * blocks_per_page, axis=1)
        for g in range(NSA_KV_HEADS):
            kvg = kv * NSA_KV_HEADS + g

            def body(dd, acc):
                r0 = kvg * HEAD_DIM + 2 * dd
                x0 = buf_ref[slot, pl.ds(r0, pg, stride=pitch), :]
                x1 = buf_ref[slot, pl.ds(r0 + 1, pg, stride=pitch), :]
                x = jnp.concatenate([x0, x1], axis=1).astype(BF16)
                return acc + _dot(x, wbuf_ref[kv, dd])
            acc = lax.fori_loop(0, n_pairs, body, jnp.zeros((pg, w2_ref.shape[1]), F32), unroll=8)
            width = w2_ref.shape[2]
            o_ref[:, kvg * width:(kvg + 1) * width] = _dot(_gelu_tanh(acc + pew).astype(BF16), w2_ref[kv])


def _pack_compress_pages(pe, w1, w2):
    d = HEAD_DIM
    nblk = PAGE_SIZE // CMP_BLOCK
    hid = w1.shape[-1]
    eye = jnp.eye(nblk, dtype=w1.dtype)
    w1r = w1.reshape(2, CMP_BLOCK, d, hid)
    wexp = jnp.einsum('kjdc,nm->kdnjmc', w1r, eye).reshape(2, d // 2, 2 * PAGE_SIZE, nblk * hid)
    w2bd = jnp.einsum('kce,nm->kncme', w2, eye).reshape(2, nblk * hid, nblk * w2.shape[-1])
    pe_rows = jnp.pad(pe.reshape(2, 1, CMP_BLOCK * d), ((0, 0), (0, 7), (0, 0)))
    return wexp.astype(BF16), pe_rows, w1.astype(BF16), w2bd.astype(BF16)


def _compress_pages(page_table_flat, pool_slabs, wexp, pe_rows, w1b, w2bd, *, base, pg):
    n_pages = page_table_flat.shape[0]
    slab = pool_slabs.shape[1]
    width = w2bd.shape[2]
    full3 = lambda a: pl.BlockSpec(a.shape, lambda i, pt: (0, 0, 0))
    return pl.pallas_call(
        functools.partial(_cmp_pages_kernel, pg=pg, base=base),
        out_shape=jax.ShapeDtypeStruct((n_pages, 2 * NSA_KV_HEADS * width), F32),
        grid_spec=pltpu.PrefetchScalarGridSpec(
            num_scalar_prefetch=1,
            grid=(n_pages // pg,),
            in_specs=[pl.BlockSpec(memory_space=pl.ANY), pl.BlockSpec(memory_space=pl.ANY),
                      full3(pe_rows), full3(w1b), full3(w2bd)],
            out_specs=pl.BlockSpec((pg, 2 * NSA_KV_HEADS * width), lambda i, pt: (i, 0)),
            scratch_shapes=[pltpu.VMEM((2, pg * (slab + 8), pool_slabs.shape[2]), F32), pltpu.VMEM(wexp.shape, BF16),
                            pltpu.SemaphoreType.DMA((3,))]),
        compiler_params=_cparams(("arbitrary",)),
        name="compress_pages",
    )(page_table_flat, pool_slabs, wexp, pe_rows, w1b, w2bd)


def _heads_to_lanes(o, ts):
    return jnp.concatenate([o[r * ts:(r + 1) * ts] for r in range(NSA_REP)], axis=1)


def _nsa_sample_a_kernel(q_ref, kcp_ref, kcn_ref, wc_ref, wn_ref, ocmp_ref, owin_ref, idx_ref, wout_ref,
                         *, past, n_new_blk, tv):
    ts = q_ref.shape[0]
    q = q_ref[...].astype(F32)
    hp = kcp_ref.shape[0]
    bpp = PAGE_SIZE // CMP_BLOCK
    plane_w = bpp * HEAD_DIM
    nbp = hp * bpp
    nsb_past = nbp * CMP_BLOCK // SEL_BLOCK
    nsb = nsb_past + (n_new_blk + 1) // 2
    wb = wc_ref.shape[1]
    m_rows = NSA_REP * ts
    row = lax.broadcasted_iota(jnp.int32, (m_rows, 1), 0)
    tpos = past + lax.rem(row, ts)
    t8 = past + lax.broadcasted_iota(jnp.int32, (ts, 1), 0)
    k_new = kcn_ref[...]
    w_old = wc_ref[...]
    w_new = wn_ref[...]
    nn = k_new.shape[0]
    pagei = lax.broadcasted_iota(jnp.int32, (1, hp), 1)
    cn = lax.broadcasted_iota(jnp.int32, (1, nn), 1)
    mask_p = jnp.concatenate([(bpp * pagei + n + 1) * CMP_BLOCK - 1 <= tpos for n in range(bpp)], axis=1)
    mask_n = ((nbp + cn + 1) * CMP_BLOCK - 1 <= tpos) & (cn < n_new_blk)
    kp_old = past - wb + lax.broadcasted_iota(jnp.int32, (1, wb), 1)
    kp_new = past + lax.broadcasted_iota(jnp.int32, (1, w_new.shape[0]), 1)
    wmask_old = (kp_old <= tpos) & (kp_old > tpos - WINDOW) & (kp_old >= 0)
    wmask_new = (kp_new <= tpos) & (kp_new > tpos - WINDOW)
    lane = lax.broadcasted_iota(jnp.int32, (1, LANES), 1)
    j = jnp.concatenate([2 * pagei, 2 * pagei + 1, nsb_past + lane], axis=1)
    width = 2 * hp + LANES

    new_t = jnp.concatenate([w_new, jnp.zeros((LANES - w_new.shape[0], w_new.shape[1]), F32)], axis=0).T
    wout_ref[...] = jnp.concatenate([w_old[:, tv:], new_t[:, 0:tv]], axis=1)

    def joint_softmax(parts):
        mx = None
        for s, mk in parts:
            cur = jnp.max(jnp.where(mk, s, NEG), axis=-1, keepdims=True)
            mx = cur if mx is None else jnp.maximum(mx, cur)
        mx = jnp.where(mx > 0.5 * NEG, mx, 0.0)
        es = [jnp.where(mk, jnp.exp(jnp.where(mk, s, NEG) - mx), 0.0) for s, mk in parts]
        den = sum(jnp.sum(e, axis=-1, keepdims=True) for e in es)
        inv = 1.0 / jnp.where(den > 0.0, den, 1.0)
        return [e * inv for e in es]

    def fold_heads(p):
        out = p[0:ts]
        for r in range(1, NSA_REP):
            out = out + p[r * ts:(r + 1) * ts]
        return out

    for g in range(NSA_KV_HEADS):
        qg = _stack_heads(q, g).astype(BF16)
        kcol = slice(g * HEAD_DIM, (g + 1) * HEAD_DIM)
        vcol = slice((NSA_KV_HEADS + g) * HEAD_DIM, (NSA_KV_HEADS + g + 1) * HEAD_DIM)

        kplanes = [kcp_ref[:, g * plane_w + n * HEAD_DIM:g * plane_w + (n + 1) * HEAD_DIM].astype(BF16)
                   for n in range(bpp)]
        vplanes = [kcp_ref[:, (NSA_KV_HEADS + g) * plane_w + n * HEAD_DIM:
                           (NSA_KV_HEADS + g) * plane_w + (n + 1) * HEAD_DIM].astype(BF16) for n in range(bpp)]
        s_past = jnp.concatenate([_dot_nt(qg, kp) for kp in kplanes], axis=1)
        p_p, p_n = joint_softmax([(s_past, mask_p), (_dot_nt(qg, k_new[:, kcol].astype(BF16)), mask_n)])
        o_cmp = _dot(p_n.astype(BF16), k_new[:, vcol].astype(BF16))
        for n in range(bpp):
            o_cmp = o_cmp + _dot(p_p[:, n * hp:(n + 1) * hp].astype(BF16), vplanes[n])
        ocmp_ref[:, g * NSA_REP * HEAD_DIM:(g + 1) * NSA_REP * HEAD_DIM] = _heads_to_lanes(o_cmp, ts)

        pp = fold_heads(p_p)
        pn = fold_heads(p_n)
        imp_new = jnp.zeros((ts, LANES), F32)
        for c in range(n_new_blk):
            imp_new = imp_new + jnp.where(lane == c // 2, pn[:, c:c + 1], 0.0)
        imp = jnp.concatenate([pp[:, 0:hp] + pp[:, hp:2 * hp], pp[:, 2 * hp:3 * hp] + pp[:, 3 * hp:4 * hp],
                               imp_new], axis=1)
        cur = t8 // SEL_BLOCK
        forced = (j == 0) | (j == cur) | (j == cur - 1)
        valid = j * SEL_BLOCK <= t8
        score = jnp.where(forced, -NEG, jnp.where(valid, imp, NEG))
        alive = j < nsb
        chosen = jnp.zeros((ts, LANES), jnp.int32)
        for n in range(min(N_SELECT, nsb)):
            best = jnp.max(jnp.where(alive, score, 2.0 * NEG), axis=-1, keepdims=True)
            pick = jnp.min(jnp.where(alive & (score == best), j, nsb), axis=-1, keepdims=True)
            alive = alive & (j != pick)
            chosen = jnp.where(lane == n, pick, chosen)
        idx_ref[g] = chosen

        kt_old = w_old[g * HEAD_DIM:(g + 1) * HEAD_DIM, :].astype(BF16)
        vt_old = w_old[(NSA_KV_HEADS + g) * HEAD_DIM:(NSA_KV_HEADS + g + 1) * HEAD_DIM, :].astype(BF16)
        pw_old, pw_new = joint_softmax([(_dot(qg, kt_old), wmask_old),
                                        (_dot_nt(qg, w_new[:, kcol].astype(BF16)), wmask_new)])
        o_win = _dot_nt(pw_old.astype(BF16), vt_old) + _dot(pw_new.astype(BF16), w_new[:, vcol].astype(BF16))
        owin_ref[:, g * NSA_REP * HEAD_DIM:(g + 1) * NSA_REP * HEAD_DIM] = _heads_to_lanes(o_win, ts)


def _nsa_sample_a(q, kc_past, kc_new, win_old, win_new, *, b, past, n_new_blk, tv):
    hp = kc_past.shape[0] // b
    blk3 = lambda a: pl.BlockSpec((None,) + a.shape[1:], lambda bb: (bb, 0, 0))
    return pl.pallas_call(
        functools.partial(_nsa_sample_a_kernel, past=past, n_new_blk=n_new_blk, tv=tv),
        out_shape=(jax.ShapeDtypeStruct((b, TS, 512), F32), jax.ShapeDtypeStruct((b, TS, 512), F32),
                   jax.ShapeDtypeStruct((b, NSA_KV_HEADS, TS, LANES), jnp.int32),
                   jax.ShapeDtypeStruct(win_old.shape, F32)),
        grid=(b,),
        in_specs=[blk3(q), pl.BlockSpec((hp, kc_past.shape[1]), lambda bb: (bb, 0)), blk3(kc_new), blk3(win_old),
                  blk3(win_new)],
        out_specs=(pl.BlockSpec((None, TS, 512), lambda bb: (bb, 0, 0)),
                   pl.BlockSpec((None, TS, 512), lambda bb: (bb, 0, 0)),
                   pl.BlockSpec((None, NSA_KV_HEADS, TS, LANES), lambda bb: (bb, 0, 0, 0)),
                   blk3(win_old)),
        compiler_params=_cparams(("parallel",)),
        name="nsa_sample_a",
    )(q, kc_past, kc_new, win_old, win_new)


def _nsa_sel_kernel(idx_ref, pt_ref, pool_ref, new_ref, q_ref, ocmp_ref, owin_ref, gate_ref, o_ref,
                    buf_ref, sem_ref, *, past, base, n_pages, tv, nsel):
    bb = pl.program_id(0)
    nb = pl.num_programs(0)
    ts = q_ref.shape[0]
    page = pool_ref.shape[2]
    blocks_per_page = page // SEL_BLOCK
    nsb_past = past // SEL_BLOCK
    kv_rows = NSA_KV_HEADS * HEAD_DIM

    def block_id(b_, g, t, n):
        return idx_ref[((b_ * NSA_KV_HEADS + g) * ts + t) * nsel + n]

    def copies(src_slab, slot, g, t, n):
        lanes = pl.ds(pl.multiple_of(n * page, page), page)
        return [pltpu.make_async_copy(src_slab.at[pl.ds(kv * kv_rows + g * HEAD_DIM, HEAD_DIM), :],
                                      buf_ref.at[slot, kv, g * tv + t, :, lanes], sem_ref.at[slot])
                for kv in range(2)]

    def fetch(b_, slot):
        for g in range(NSA_KV_HEADS):
            for t in range(tv):
                def body(n, c):
                    bid = block_id(b_, g, t, n)

                    @pl.when(bid < nsb_past)
                    def _():
                        src = pool_ref.at[base + pt_ref[b_ * n_pages + bid // blocks_per_page]]
                        for cp in copies(src, slot, g, t, n):
                            cp.start()

                    @pl.when(bid >= nsb_past)
                    def _():
                        for cp in copies(new_ref.at[b_], slot, g, t, n):
                            cp.start()
                    return c
                lax.fori_loop(0, nsel, body, 0)

    @pl.when(bb == 0)
    def _():
        fetch(0, 0)

    @pl.when(bb + 1 < nb)
    def _():
        fetch(bb + 1, lax.rem(bb + 1, 2))

    slot = lax.rem(bb, 2)

    def wait_body(n, c):
        for cp in copies(pool_ref.at[0], slot, 0, 0, 0):
            cp.wait()
        return c
    lax.fori_loop(0, NSA_KV_HEADS * tv * nsel, wait_body, 0)

    q = q_ref[...].astype(F32)
    gates = gate_ref[...]
    o_cmp = ocmp_ref[...]
    o_win = owin_ref[...]
    m_rows = NSA_REP * ts
    row = lax.broadcasted_iota(jnp.int32, (m_rows, 1), 0)
    trow = lax.rem(row, ts)
    tpos = past + trow
    nk = nsel * page
    lane = lax.broadcasted_iota(jnp.int32, (1, nk), 1)
    in_page = lax.rem(lane, page)
    outs = []
    for g in range(NSA_KV_HEADS):
        qg = _stack_heads(q, g).astype(BF16)
        o_sel = jnp.zeros((m_rows, HEAD_DIM), F32)
        for t in range(tv):
            kt = buf_ref[slot, 0, g * tv + t].astype(BF16)
            vt = buf_ref[slot, 1, g * tv + t].astype(BF16)
            first_pos = jnp.zeros((1, nk), jnp.int32)
            half = jnp.zeros((1, nk), jnp.int32)
            for n in range(nsel):
                bid = block_id(bb, g, t, n)
                here = lane // page == n
                first_pos = jnp.where(here, (bid // blocks_per_page) * page, first_pos)
                half = jnp.where(here, lax.rem(bid, blocks_per_page), half)
            kpos = first_pos + in_page
            mask = (in_page // SEL_BLOCK == half) & (kpos <= tpos)
            s = jnp.where(mask, _dot(qg, kt), NEG)
            e = jnp.where(mask, jnp.exp(s - jnp.max(s, axis=-1, keepdims=True)), 0.0)
            p = e / jnp.sum(e, axis=-1, keepdims=True)
            o_sel = o_sel + jnp.where(trow == t, _dot_nt(p.astype(BF16), vt), 0.0)
        for r in range(NSA_REP):
            h = g * NSA_REP + r
            hs = slice(h * HEAD_DIM, (h + 1) * HEAD_DIM)
            outs.append(o_cmp[:, hs] * gates[:, 3 * h:3 * h + 1] + o_sel[r * ts:(r + 1) * ts] * gates[:, 3 * h + 1:3 * h + 2]
                        + o_win[:, hs] * gates[:, 3 * h + 2:3 * h + 3])
    o_ref[...] = jnp.concatenate(outs, axis=1)


def _nsa_sample_sel(idx_flat, page_table_flat, pool_slabs, new_slabs, q, o_cmp, o_win, gates, *, b, past, base, tv):
    nsel = idx_flat.shape[0] // (b * NSA_KV_HEADS * TS)
    page = pool_slabs.shape[2]
    blk3 = lambda a: pl.BlockSpec((None,) + a.shape[1:], lambda bb, i_, p_: (bb, 0, 0))
    return pl.pallas_call(
        functools.partial(_nsa_sel_kernel, past=past, base=base, n_pages=page_table_flat.shape[0] // b,
                          tv=tv, nsel=nsel),
        out_shape=jax.ShapeDtypeStruct((b, TS, 512), F32),
        grid_spec=pltpu.PrefetchScalarGridSpec(
            num_scalar_prefetch=2,
            grid=(b,),
            in_specs=[pl.BlockSpec(memory_space=pl.ANY), pl.BlockSpec(memory_space=pl.ANY),
                      blk3(q), blk3(o_cmp), blk3(o_win), blk3(gates)],
            out_specs=pl.BlockSpec((None, TS, 512), lambda bb, i_, p_: (bb, 0, 0)),
            scratch_shapes=[pltpu.VMEM((2, 2, NSA_KV_HEADS * tv, HEAD_DIM, nsel * page), F32),
                            pltpu.SemaphoreType.DMA((2,))]),
        compiler_params=_cparams(("arbitrary",)),
        name="nsa_sample_sel",
    )(idx_flat, page_table_flat, pool_slabs, new_slabs, q, o_cmp, o_win, gates)


def _layer_weights(l, norm_mix, w_in, gla_wa2, gla_ba, gla_norm, cmp_pe, cmp_w1, cmp_w2, w_out, norm_mem,
                   w_mq, w_mkv, w_mo, norm_ffn, w_ff1, w_ff2):
    pe_big, w1_big, w2_big = _pack_compress(cmp_pe[l], cmp_w1[l], cmp_w2[l])
    return dict(
        pages=_pack_compress_pages(cmp_pe[l], cmp_w1[l], cmp_w2[l]), norm_mix=norm_mix[l][None, :], w_in=_pack_w_in(w_in[l]), wa2=_pack_wa2(gla_wa2[l]),
        ba=gla_ba[l][None, :], gnorm=gla_norm[l][None, :], pe_big=pe_big, w1_big=w1_big, w2_big=w2_big,
        w_out=w_out[l].astype(BF16), norm_mem=norm_mem[l][None, :], w_mq=w_mq[l].astype(BF16),
        w_mkv=w_mkv[l].astype(BF16), w_mo=w_mo[l].astype(BF16), norm_ffn=norm_ffn[l][None, :],
        w_ff1=w_ff1[l].astype(BF16), w_ff2=w_ff2[l].astype(BF16))


def _even_odd(kcv, b):
    nb = kcv.shape[0] // b
    return kcv.reshape(b, nb // 2, 2, -1).transpose(0, 2, 1, 3).reshape(b * nb, -1)


def _prompt_layer(x, mem, p, tabs, gfinal, *, b, t, final_norm):
    q, cmp_rows, slc_rows, win_rows, gates, gla, ret = _project(
        x, p['norm_mix'], p['w_in'], p['wa2'], p['ba'], tabs, tm=256)
    kcv = _compress(cmp_rows.reshape(-1, CMP_BLOCK * KV_W), p['pe_big'], p['w1_big'], p['w2_big'], tr=128)
    o_nsa = _nsa_prompt(q, _even_odd(kcv, b), slc_rows, win_rows, gates, b=b, t=t, tq=256, tk=256)
    sg0 = jnp.zeros((b, GLA_HEADS, GLA_DK, GLA_DV), F32)
    sr0 = jnp.zeros((b, RET_HEADS, RET_DIM, RET_DIM), F32)
    o_rec, s_gla, s_ret = _recurrent(gla, ret, p['gnorm'], sg0, sr0, b=b, t=t, tt=128, nbk=2 if b % 2 == 0 else 1)
    x = _outproj(x, o_nsa, o_rec, p['w_out'], tm=512)
    mem_kv = _matmul(mem, p['w_mkv'], tm=256)
    x = _cross(x, p['norm_mem'], p['w_mq'], p['w_mo'], mem_kv, b=b, t=t, tq=512)
    x = _ffn(x, p['norm_ffn'], p['w_ff1'], p['w_ff2'], gfinal, tm=1024, tf=1024, final_norm=final_norm)
    wlen = min(WINDOW, t)
    win_tail = win_rows.reshape(b, t, KV_W)[:, t - wlen:]
    return x, (cmp_rows, slc_rows, win_tail, s_gla, s_ret, mem_kv)


def _sample_layer(x, p, tabs, gfinal, cmp_slabs, slc_slabs, win_slabs, sg0, sr0, mem_cache, pt_flat,
                  *, b, past, tv, base, layer, final_norm):
    n = b * TS
    q, cmp_rows, slc_rows, win_rows, gates, gla, ret = _project(
        x, p['norm_mix'], p['w_in'], p['wa2'], p['ba'], tabs, tm=n, valid_period=(TS, tv))
    rows3 = lambda a: a.reshape(b, TS, a.shape[-1])
    pad_blk = lambda a: jnp.pad(rows3(a), ((0, 0), (0, SEL_BLOCK - TS), (0, 0)))
    n_new_blk = SEL_BLOCK // CMP_BLOCK
    pg = min(128, past // PAGE_SIZE)
    kc_past = _compress_pages(pt_flat, cmp_slabs, *p['pages'], base=base, pg=pg)
    kc_new = _compress(pad_blk(cmp_rows).reshape(b * n_new_blk, CMP_BLOCK * KV_W), p['pe_big'], p['w1_big'],
                       p['w2_big'], tr=b * n_new_blk)
    kc_new = jnp.pad(kc_new.reshape(b, n_new_blk, KV_W), ((0, 0), (0, TS - n_new_blk), (0, 0)))
    q3 = rows3(q)
    o_cmp, o_win, idx, new_win = _nsa_sample_a(q3, kc_past, kc_new, win_slabs, rows3(win_rows), b=b, past=past,
                                               n_new_blk=n_new_blk, tv=tv)
    nsel = min(N_SELECT, past // SEL_BLOCK + 1)
    slc_new = jnp.pad(jnp.swapaxes(rows3(slc_rows), 1, 2), ((0, 0), (0, 0), (0, cmp_slabs.shape[2] - TS)))
    o_nsa = _nsa_sample_sel(idx[..., :nsel].reshape(-1), pt_flat, slc_slabs, slc_new, q3, o_cmp, o_win,
                            rows3(gates), b=b, past=past, base=base, tv=tv)
    o_rec, s_gla, s_ret = _recurrent(gla, ret, p['gnorm'], sg0, sr0, b=b, t=TS, tt=TS, nbk=2 if b % 2 == 0 else 1, n_valid=tv)
    x = _outproj(x, o_nsa.reshape(n, -1), o_rec, p['w_out'], tm=n)
    x = _cross(x, p['norm_mem'], p['w_mq'], p['w_mo'], mem_cache, b=b, t=TS, tq=TS, layer=layer)
    x = _ffn(x, p['norm_ffn'], p['w_ff1'], p['w_ff2'], gfinal, tm=n, tf=1024, final_norm=final_norm)
    return x, (rows3(cmp_rows)[:, :tv], rows3(slc_rows)[:, :tv], new_win, s_gla, s_ret)


def kernel(x_prompt, x_sample, mem_prompt, cache_cmp_kv, cache_slc_kv, cache_win_kv, state_gla, state_ret,
           cache_mem_kv, page_table, norm_mix, w_in, gla_wa2, gla_ba, gla_norm, cmp_pe, cmp_w1, cmp_w2, w_out,
           norm_mem, w_mq, w_mkv, w_mo, norm_ffn, w_ff1, w_ff2, norm_final):
    depth = w_in.shape[0]
    b, t, d = x_prompt.shape
    g, hd = NSA_KV_HEADS, HEAD_DIM
    gfinal = norm_final[None, :]
    tabs_p = _all_rope_tables(jnp.arange(t))
    hp = x_prompt.reshape(b * t, d)
    mem = mem_prompt.reshape(-1, d)
    sb, tv = x_sample.shape[:2]
    assert tv <= TS
    past = page_table.shape[1] * PAGE_SIZE
    tabs_s = _all_rope_tables(past + jnp.arange(sb * TS) % TS)
    hs = jnp.pad(x_sample, ((0, 0), (0, TS - tv), (0, 0))).reshape(sb * TS, d)
    pt_flat = page_table.reshape(-1)
    n_pool = cache_cmp_kv.shape[1]
    cmp_slabs = _pool_slabs(cache_cmp_kv)
    slc_slabs = _pool_slabs(cache_slc_kv)
    win_slabs = jnp.transpose(cache_win_kv, (0, 1, 3, 4, 5, 2)).reshape(depth, sb, KV_W, -1)
    new_p = [[] for _ in range(6)]
    new_s = [[] for _ in range(5)]
    for l in range(depth):
        p = _layer_weights(l, norm_mix, w_in, gla_wa2, gla_ba, gla_norm, cmp_pe, cmp_w1, cmp_w2, w_out, norm_mem,
                           w_mq, w_mkv, w_mo, norm_ffn, w_ff1, w_ff2)
        last = l == depth - 1
        hp, st = _prompt_layer(hp, mem, p, tabs_p, gfinal, b=b, t=t, final_norm=last)
        for lst, a in zip(new_p, st):
            lst.append(a)
        hs, st = _sample_layer(hs, p, tabs_s, gfinal, cmp_slabs, slc_slabs, win_slabs[l], state_gla[l], state_ret[l],
                               cache_mem_kv, pt_flat, b=sb, past=past, tv=tv, base=l * n_pool, layer=l,
                               final_norm=last)
        for lst, a in zip(new_s, st):
            lst.append(a)
    cmp_p, slc_p, win_p, gla_p, ret_p, mem_p = [jnp.stack(a) for a in new_p]
    cmp_s, slc_s, win_s, gla_s, ret_s = [jnp.stack(a) for a in new_s]
    n_mem = mem_prompt.shape[1]
    kv6 = lambda a: a.reshape(a.shape[:3] + (2, g, hd))
    win_s = jnp.transpose(win_s.reshape(depth, sb, 2, g, hd, -1), (0, 1, 5, 2, 3, 4))
    return (hp.reshape(b, t, d), hs.reshape(sb, TS, d)[:, :tv],
            kv6(cmp_p.reshape(depth, b, t, KV_W)), kv6(slc_p.reshape(depth, b, t, KV_W)), kv6(win_p), gla_p, ret_p,
            mem_p.reshape(depth, b, n_mem, 2, MEM_HEADS, MEM_HD),
            kv6(cmp_s), kv6(slc_s), win_s, gla_s, ret_s)
```

```python
import functools
import math

import numpy as np
import jax
import jax.numpy as jnp
from jax import lax
from jax.experimental import pallas as pl
from jax.experimental.pallas import tpu as pltpu

F32 = jnp.float32
BF16 = jnp.bfloat16

D_MODEL = 1024
PAGE_SIZE = 128
HEAD_DIM = 64
NSA_HEADS = 8
NSA_KV_HEADS = 2
NSA_REP = NSA_HEADS // NSA_KV_HEADS
CMP_BLOCK = 32
SEL_BLOCK = 64
N_SELECT = 16
WINDOW = 512
ROT_DIM = HEAD_DIM // 4
ROPE_THETA = 500000.0
GLA_HEADS = 4
GLA_DK = 32
GLA_DV = 64
GLA_RANK = 16
GLA_TAU = 16.0
RET_HEADS = 4
RET_DIM = 64
RET_THETA = 10000.0
MEM_HEADS = 4
MEM_HD = D_MODEL // MEM_HEADS
D_FF = 4 * D_MODEL
EPS = 1e-6
IN_SIZES = (NSA_HEADS * HEAD_DIM, 6 * NSA_KV_HEADS * HEAD_DIM, 3 * NSA_HEADS,
            GLA_HEADS * GLA_DK, GLA_HEADS * GLA_DK, GLA_HEADS * GLA_DV, GLA_RANK, GLA_HEADS * GLA_DV,
            RET_HEADS * RET_DIM, RET_HEADS * RET_DIM, RET_HEADS * RET_DIM, RET_HEADS * RET_DIM)

LANES = 128
VMEM_LIMIT = 56 << 20
NEG = -1e30
M_INIT = -1e29
KV_W = 2 * NSA_KV_HEADS * HEAD_DIM

C_Q = 0
C_KV = 512
C_GLA = 1280
C_RET = 2048
C_TAIL = 3072
W_IN_COLS = 3200
GLA_W = 896
RET_W = 1024


def _cparams(sem, vmem=VMEM_LIMIT):
    return pltpu.CompilerParams(dimension_semantics=sem, vmem_limit_bytes=vmem)


def _dot(a, b):
    return jnp.dot(a, b, preferred_element_type=F32)


def _dot_nt(a, b):
    return lax.dot_general(a, b, (((1,), (1,)), ((), ())), preferred_element_type=F32)


def _dot_tn(a, b):
    return lax.dot_general(a, b, (((0,), (0,)), ((), ())), preferred_element_type=F32)


def _split3(x):
    hi = x.astype(BF16)
    r = x - hi.astype(F32)
    mid = r.astype(BF16)
    lo = (r - mid.astype(F32)).astype(BF16)
    return hi, mid, lo


def _rms(x, eps=EPS):
    return x * lax.rsqrt(jnp.mean(x * x, axis=-1, keepdims=True) + eps)


def _silu(x):
    return x * jax.nn.sigmoid(x)


def _rope128(v, c, sa, sb, half):
    return v * c + pltpu.roll(v, LANES - half, 1) * sa + pltpu.roll(v, half, 1) * sb


def _proj_kernel(x_ref, gain_ref, w_ref, wa2_ref, ba_ref, nc_ref, nsa_ref, nsb_ref, rc_ref, rsa_ref, rsb_ref,
                 q_ref, cmp_ref, slc_ref, win_ref, gate_ref, gla_ref, ret_ref, *, valid_period):
    x = x_ref[...]
    xn = _rms(x) * gain_ref[...]
    if valid_period is not None:
        period, n_valid = valid_period
        row = lax.broadcasted_iota(jnp.int32, (x.shape[0], 1), 0)
        xn = jnp.where(lax.rem(row, period) < n_valid, xn, 0.0)
    xb = xn.astype(BF16)

    def mm(a, b):
        return _dot(xb, w_ref[:, a:b])

    nc, nsa, nsb = nc_ref[...], nsa_ref[...], nsb_ref[...]
    rc, rsa, rsb = rc_ref[...], rsa_ref[...], rsb_ref[...]
    half_n = ROT_DIM // 2
    half_r = RET_DIM // 2

    q = mm(C_Q, C_KV)
    for c in range(4):
        sl = slice(c * LANES, (c + 1) * LANES)
        q_ref[:, sl] = (_rope128(q[:, sl], nc, nsa, nsb, half_n) * (HEAD_DIM ** -0.5)).astype(q_ref.dtype)

    kv = mm(C_KV, C_GLA)
    for br, ref in enumerate((cmp_ref, slc_ref, win_ref)):
        ref[:, 0:LANES] = _rope128(kv[:, br * KV_W:br * KV_W + LANES], nc, nsa, nsb, half_n)
        ref[:, LANES:KV_W] = kv[:, br * KV_W + LANES:(br + 1) * KV_W]

    tail = mm(C_TAIL, W_IN_COLS)
    gate_ref[...] = jax.nn.sigmoid(tail)
    z = jnp.dot(tail, wa2_ref[...], preferred_element_type=F32, precision=lax.Precision.HIGHEST) + ba_ref[...]
    log_a = (jnp.minimum(z, 0.0) - jnp.log1p(jnp.exp(-jnp.abs(z)))) * (1.0 / GLA_TAU)

    gl = mm(C_GLA, C_RET)
    gla_ref[:, 0:128] = gl[:, 0:128] * (GLA_DK ** -0.5)
    gla_ref[:, 128:256] = gl[:, 128:256]
    gla_ref[:, 256:384] = log_a
    gla_ref[:, 384:640] = gl[:, 256:512]
    gla_ref[:, 640:896] = _silu(gl[:, 512:768])

    rt = mm(C_RET, C_TAIL)
    for c in range(2):
        sl = slice(c * LANES, (c + 1) * LANES)
        ret_ref[:, sl] = _rope128(rt[:, sl], rc, rsa, rsb, half_r)
        sk = slice(256 + c * LANES, 256 + (c + 1) * LANES)
        ret_ref[:, sk] = _rope128(rt[:, sk], rc, rsa, rsb, half_r) * (RET_DIM ** -0.5)
    ret_ref[:, 512:768] = rt[:, 512:768]
    ret_ref[:, 768:1024] = _silu(rt[:, 768:1024])


def _project(x, gain, w, wa2, ba, tabs, *, tm, valid_period=None):
    n = x.shape[0]
    p = tabs[0].shape[0]
    nt = p // tm
    row = lambda w_: pl.BlockSpec((tm, w_), lambda i: (i, 0))
    full = lambda a: pl.BlockSpec(a.shape, lambda i: (0, 0))
    tab = pl.BlockSpec((tm, LANES), lambda i: (i % nt, 0))
    out_shape = (jax.ShapeDtypeStruct((n, 512), BF16),
                 jax.ShapeDtypeStruct((n, KV_W), F32), jax.ShapeDtypeStruct((n, KV_W), F32),
                 jax.ShapeDtypeStruct((n, KV_W), F32), jax.ShapeDtypeStruct((n, LANES), F32),
                 jax.ShapeDtypeStruct((n, GLA_W), F32), jax.ShapeDtypeStruct((n, RET_W), F32))
    return pl.pallas_call(
        functools.partial(_proj_kernel, valid_period=valid_period),
        out_shape=out_shape,
        grid=(n // tm,),
        in_specs=[row(D_MODEL), full(gain), full(w), full(wa2), full(ba)] + [tab] * 6,
        out_specs=(row(512), row(KV_W), row(KV_W), row(KV_W), row(LANES), row(GLA_W), row(RET_W)),
        compiler_params=_cparams(("parallel",)),
        name="proj",
    )(x, gain, w, wa2, ba, *tabs)


def _rope_tables(pos, inv_freq):
    nf = inv_freq.shape[0]
    ang = pos.astype(F32)[:, None] * inv_freq[None, :]
    cos, sin = jnp.cos(ang), jnp.sin(ang)
    ones = jnp.ones((pos.shape[0], HEAD_DIM - 2 * nf), F32)
    zeros = jnp.zeros((pos.shape[0], HEAD_DIM - nf), F32)
    c = jnp.concatenate([cos, cos, ones], axis=1)
    sa = jnp.concatenate([-sin, zeros], axis=1)
    sb = jnp.concatenate([zeros[:, :nf], sin, zeros[:, :HEAD_DIM - 2 * nf]], axis=1)
    return tuple(jnp.tile(t, (1, LANES // HEAD_DIM)) for t in (c, sa, sb))


def _all_rope_tables(pos):
    nsa_f = ROPE_THETA ** (-jnp.arange(0, ROT_DIM, 2, dtype=F32) / ROT_DIM)
    ret_f = RET_THETA ** (-jnp.linspace(0.0, 1.0, RET_DIM // 2, dtype=F32))
    return _rope_tables(pos, nsa_f) + _rope_tables(pos, ret_f)


def _pack_w_in(w):
    offs = np.cumsum((0,) + IN_SIZES)
    nq, nkv, ngate, gq, gk, gv, ga, gr, rq, rk, rv, rg = [w[:, offs[i]:offs[i + 1]] for i in range(12)]
    pad = jnp.zeros((w.shape[0], W_IN_COLS - C_TAIL - ngate.shape[1] - ga.shape[1]), w.dtype)
    return jnp.concatenate([nq, nkv, gq, gk, gv, gr, rq, rk, rv, rg, ngate, ga, pad], axis=1).astype(BF16)


def _pack_wa2(wa2):
    top = jnp.zeros((3 * NSA_HEADS, wa2.shape[1]), wa2.dtype)
    bot = jnp.zeros((LANES - 3 * NSA_HEADS - GLA_RANK, wa2.shape[1]), wa2.dtype)
    return jnp.concatenate([top, wa2, bot], axis=0)


def _gelu_tanh(x):
    return 0.5 * x * (1.0 + jnp.tanh(0.7978845608028654 * (x + 0.044715 * (x * x * x))))


def _compress_kernel(x_ref, pe_ref, w1_ref, w2_ref, o_ref):
    xb = (x_ref[...] + pe_ref[...]).astype(BF16)
    h = _gelu_tanh(_dot(xb, w1_ref[...]))
    o_ref[...] = _dot(h.astype(BF16), w2_ref[...])


def _compress(rows_flat, pe_big, w1_big, w2_big, *, tr):
    r, k = rows_flat.shape
    tr = min(tr, r)
    full = lambda a: pl.BlockSpec(a.shape, lambda i: (0, 0))
    return pl.pallas_call(
        _compress_kernel,
        out_shape=jax.ShapeDtypeStruct((r, KV_W), F32),
        grid=(r // tr,),
        in_specs=[pl.BlockSpec((tr, k), lambda i: (i, 0)), full(pe_big), full(w1_big), full(w2_big)],
        out_specs=pl.BlockSpec((tr, KV_W), lambda i: (i, 0)),
        compiler_params=_cparams(("parallel",)),
        name="compress",
    )(rows_flat, pe_big, w1_big, w2_big)


def _pack_compress(pe, w1, w2):
    g, d = NSA_KV_HEADS, HEAD_DIM
    hid = w1.shape[-1]
    pe_big = jnp.broadcast_to(pe.transpose(1, 0, 2)[:, :, None, :], (CMP_BLOCK, 2, g, d)).reshape(1, -1)
    eye = jnp.eye(2 * g, dtype=w1.dtype).reshape(2, g, 2, g)
    w1r = w1.reshape(2, CMP_BLOCK, d, hid)
    w1_big = jnp.einsum('kjdc,kgKG->jkgdKGc', w1r, eye).reshape(CMP_BLOCK * KV_W, 2 * g * hid)
    w2_big = jnp.einsum('kce,kgKG->kgcKGe', w2, eye).reshape(2 * g * hid, KV_W)
    return pe_big, w1_big.astype(BF16), w2_big.astype(BF16)


def _stack_heads(q, g):
    return jnp.concatenate([q[:, (g * NSA_REP + r) * HEAD_DIM:(g * NSA_REP + r + 1) * HEAD_DIM]
                            for r in range(NSA_REP)], axis=0)


def _flash_init(n_chains, nq):
    return tuple((jnp.full((1, nq), M_INIT, F32), jnp.zeros((1, nq), F32), jnp.zeros((HEAD_DIM, nq), F32))
                 for _ in range(n_chains))


def _flash_finish(carry):
    return [acc / l_fin for _, l_fin, acc in carry]


def _flash_t(chains, tpos, lo, hi, tk, carry):
    def body(kt, carry):
        k0 = pl.multiple_of(kt * tk, tk)
        kpos = k0 + lax.broadcasted_iota(jnp.int32, (tk, 1), 0)
        out = []
        for (kv_ref, kcol, vcol, q_t, mask_fn), (m_old, l_old, acc) in zip(chains, carry):
            kb = kv_ref[pl.ds(k0, tk), kcol].astype(BF16)
            vb = kv_ref[pl.ds(k0, tk), vcol].astype(BF16)
            s_t = mask_fn(_dot(kb, q_t), k0, kpos, tpos)
            m_new = jnp.maximum(m_old, jnp.max(s_t, axis=0, keepdims=True))
            alpha = jnp.exp(m_old - m_new)
            p = jnp.exp(s_t - m_new)
            l_new = alpha * l_old + jnp.sum(p, axis=0, keepdims=True)
            out.append((m_new, l_new, alpha * acc + _dot_tn(vb, p.astype(BF16))))
        return tuple(out)

    return lax.fori_loop(lo, hi, body, carry)


def _nsa_prompt_kernel(q_ref, kcv_ref, slc_ref, win_ref, gate_ref, exp_ref, o_ref, selx_ref, *, tq, tk):
    i = pl.program_id(1)
    mq = NSA_REP * tq
    nb = kcv_ref.shape[0]
    nsb = nb // 2
    q_t = q_ref[...].astype(F32).T
    gate_t = gate_ref[...].T
    tcol = i * tq + lax.rem(lax.broadcasted_iota(jnp.int32, (1, mq), 1), tq)
    tq_col = i * tq + lax.broadcasted_iota(jnp.int32, (1, tq), 1)
    blk_row = lax.broadcasted_iota(jnp.int32, (nb, 1), 0)
    blk = jnp.where(blk_row < nsb, 2 * blk_row, 2 * (blk_row - nsb) + 1)
    cmask = (blk + 1) * CMP_BLOCK - 1 <= tcol
    jrow = lax.broadcasted_iota(jnp.int32, (nsb, 1), 0)
    cur = tq_col // SEL_BLOCK
    forced = (jrow == 0) | (jrow == cur) | (jrow == cur - 1)
    valid = jrow * SEL_BLOCK <= tq_col
    n_kt = ((i + 1) * tq + tk - 1) // tk
    win_lo = jnp.maximum(i * tq - WINDOW + 1, 0) // tk

    def sel_mask(g, causal):
        def fn(s, k0, kpos, tpos):
            bias = selx_ref[g, pl.ds(k0, tk), :]
            s = s + jnp.concatenate([bias] * NSA_REP, axis=1)
            return jnp.where(kpos <= tpos, s, NEG) if causal else s
        return fn

    def win_mask(s, k0, kpos, tpos):
        return jnp.where(kpos <= tpos, jnp.where(kpos > tpos - WINDOW, s, NEG), NEG)

    kcols = [slice(g * HEAD_DIM, (g + 1) * HEAD_DIM) for g in range(NSA_KV_HEADS)]
    vcols = [slice((NSA_KV_HEADS + g) * HEAD_DIM, (NSA_KV_HEADS + g + 1) * HEAD_DIM) for g in range(NSA_KV_HEADS)]
    qgs, o_cmps = [], []
    for g in range(NSA_KV_HEADS):
        qg = jnp.concatenate([q_t[(g * NSA_REP + r) * HEAD_DIM:(g * NSA_REP + r + 1) * HEAD_DIM, :]
                              for r in range(NSA_REP)], axis=1).astype(BF16)
        qgs.append(qg)
        kcol, vcol = kcols[g], vcols[g]

        s_t = _dot(kcv_ref[:, kcol].astype(BF16), qg)
        sm = jnp.where(cmask, s_t, NEG)
        mx = jnp.max(sm, axis=0, keepdims=True)
        mx = jnp.where(mx > 0.5 * NEG, mx, 0.0)
        e = jnp.where(cmask, jnp.exp(sm - mx), 0.0)
        den = jnp.sum(e, axis=0, keepdims=True)
        p = e / jnp.where(den > 0.0, den, 1.0)
        o_cmp = _dot_tn(kcv_ref[:, vcol].astype(BF16), p.astype(BF16))
        ps = p[:, 0:tq]
        for r in range(1, NSA_REP):
            ps = ps + p[:, r * tq:(r + 1) * tq]
        imp = ps[0:nsb] + ps[nsb:nb]

        score = jnp.where(forced, -NEG, jnp.where(valid, imp, NEG))
        rank = jnp.zeros((nsb, tq), F32)
        for c in range(nsb):
            sc = score[c:c + 1, :]
            beats = (sc > score) | ((sc == score) & (jrow > c))
            rank = rank + jnp.where(beats, 1.0, 0.0)
        sel = jnp.where(rank < float(N_SELECT), 1.0, 0.0).astype(BF16)
        selx_ref[g] = (_dot(exp_ref[...], sel) - 1.0) * (-NEG)
        o_cmps.append(o_cmp)

    groups = range(NSA_KV_HEADS)
    carry = _flash_t([(slc_ref, kcols[g], vcols[g], qgs[g], sel_mask(g, False)) for g in groups], tcol,
                     0, win_lo, tk, _flash_init(NSA_KV_HEADS, mq))
    carry = _flash_t([(slc_ref, kcols[g], vcols[g], qgs[g], sel_mask(g, True)) for g in groups]
                     + [(win_ref, kcols[g], vcols[g], qgs[g], win_mask) for g in groups], tcol,
                     win_lo, n_kt, tk, carry + _flash_init(NSA_KV_HEADS, mq))
    o_sels = _flash_finish(carry[:NSA_KV_HEADS])
    o_wins = _flash_finish(carry[NSA_KV_HEADS:])

    for g in groups:
        outs = []
        for r in range(NSA_REP):
            h = g * NSA_REP + r
            cs = slice(r * tq, (r + 1) * tq)
            outs.append(o_cmps[g][:, cs] * gate_t[3 * h:3 * h + 1, :]
                        + o_sels[g][:, cs] * gate_t[3 * h + 1:3 * h + 2, :]
                        + o_wins[g][:, cs] * gate_t[3 * h + 2:3 * h + 3, :])
        o_ref[:, g * NSA_REP * HEAD_DIM:(g + 1) * NSA_REP * HEAD_DIM] = jnp.concatenate(outs, axis=0).T.astype(o_ref.dtype)


def _nsa_prompt(q, kcv, slc, win, gates, *, b, t, tq, tk):
    n = b * t
    nq = t // tq
    nb = t // CMP_BLOCK
    nsb = t // SEL_BLOCK
    tk = min(tk, t)
    expand = (np.arange(t)[:, None] // SEL_BLOCK == np.arange(nsb)[None, :]).astype(np.float32)
    expand = jnp.asarray(expand, BF16)
    return pl.pallas_call(
        functools.partial(_nsa_prompt_kernel, tq=tq, tk=tk),
        out_shape=jax.ShapeDtypeStruct((n, NSA_HEADS * HEAD_DIM), BF16),
        grid=(b, nq),
        in_specs=[pl.BlockSpec((tq, 512), lambda bb, i: (bb * nq + i, 0)),
                  pl.BlockSpec((nb, KV_W), lambda bb, i: (bb, 0)),
                  pl.BlockSpec((t, KV_W), lambda bb, i: (bb, 0)),
                  pl.BlockSpec((t, KV_W), lambda bb, i: (bb, 0)),
                  pl.BlockSpec((tq, LANES), lambda bb, i: (bb * nq + i, 0)),
                  pl.BlockSpec((t, nsb), lambda bb, i: (0, 0))],
        out_specs=pl.BlockSpec((tq, 512), lambda bb, i: (bb * nq + i, 0)),
        scratch_shapes=[pltpu.VMEM((NSA_KV_HEADS, t, tq), F32)],
        compiler_params=_cparams(("parallel", "arbitrary")),
        name="nsa_prompt",
    )(q, kcv, slc, win, gates, expand)


def _gla_level_matrix(tt):
    lv = int(math.log2(tt))
    m = np.zeros((lv, tt, tt), np.float32)
    t = np.arange(tt)
    for l in range(lv):
        half = 1 << l
        split = ((t >> (l + 1)) << (l + 1)) + half - 1
        u = np.arange(tt)[None, :]
        upper = t > split
        m[l] = np.where(upper[:, None], (u > split[:, None]) & (u <= t[:, None]),
                        (u > t[:, None]) & (u <= split[:, None]))
    return m.reshape(lv * tt, tt)


def _ret_log_decay():
    return [float(np.log(np.float32(1.0) - np.float32(2.0) ** np.float32(-5.0 - h))) for h in range(RET_HEADS)]


def _recur_kernel(gla_ref, ret_ref, gn_ref, lvl_ref, tril_ref, sg0_ref, sr0_ref,
                  o_ref, sg_out_ref, sr_out_ref, sg_ref, sr_ref, *, tt, n_valid):
    ti = pl.program_id(1)
    nt = pl.num_programs(1)

    @pl.when(ti == 0)
    def _():
        sg_ref[...] = sg0_ref[...]
        sr_ref[...] = sr0_ref[...]

    for bi in range(gla_ref.shape[0]):
        _recur_tile(gla_ref.at[bi], ret_ref.at[bi], gn_ref, lvl_ref, tril_ref, o_ref.at[bi], sg_ref.at[bi],
                    sr_ref.at[bi], tt=tt, n_valid=n_valid)

    @pl.when(ti == nt - 1)
    def _():
        sg_out_ref[...] = sg_ref[...]
        sr_out_ref[...] = sr_ref[...]


def _recur_tile(gla_ref, ret_ref, gn_ref, lvl_ref, tril_ref, o_ref, sg_ref, sr_ref, *, tt, n_valid):
    levels = int(math.log2(tt))
    rowi = lax.broadcasted_iota(jnp.int32, (tt, 1), 0)
    coli = lax.broadcasted_iota(jnp.int32, (1, tt), 1)

    gq = gla_ref[:, 0:128]
    gk = gla_ref[:, 128:256]
    la = gla_ref[:, 256:384]
    gv = gla_ref[:, 384:640]
    gr = gla_ref[:, 640:896]
    if n_valid < tt:
        la = jnp.where(rowi < n_valid, la, 0.0)
    parts = _split3(la)
    tril = tril_ref[...]
    lvl = lvl_ref[...]
    ones = jnp.ones((tt, GLA_DV), BF16)
    cum = sum(_dot(tril, pt) for pt in parts)
    dlv = sum(_dot(lvl, pt) for pt in parts)
    tot = sum(_dot_tn(pt, ones) for pt in parts)
    q_dec = gq * jnp.exp(cum)
    k_dec = gk * jnp.exp(cum[tt - 1:tt, :] - cum)
    gla_out = []
    for h in range(GLA_HEADS):
        sl = slice(h * GLA_DK, (h + 1) * GLA_DK)
        vs = slice(h * GLA_DV, (h + 1) * GLA_DV)
        qh, kh = gq[:, sl], gk[:, sl]
        vh = gv[:, vs].astype(BF16)
        attn = jnp.where(rowi == coli, _dot_nt(qh.astype(BF16), kh.astype(BF16)), 0.0)
        for l in range(levels):
            ed = jnp.exp(dlv[l * tt:(l + 1) * tt, sl])
            upper = ((rowi >> l) & 1) == 1
            qe = jnp.where(upper, qh * ed, 0.0).astype(BF16)
            ke = jnp.where(upper, 0.0, kh * ed).astype(BF16)
            same = (rowi >> (l + 1)) == (coli >> (l + 1))
            attn = attn + jnp.where(same, _dot_nt(qe, ke), 0.0)
        s_h = sg_ref[h]
        o = _dot(attn.astype(BF16), vh) + _dot(q_dec[:, sl].astype(BF16), s_h.astype(BF16))
        sg_ref[h] = jnp.exp(tot[sl, :]) * s_h + _dot_tn(k_dec[:, sl].astype(BF16), vh)
        gla_out.append(_rms(o) * gn_ref[...] * gr[:, vs])
    o_ref[:, 0:256] = jnp.concatenate(gla_out, axis=1).astype(o_ref.dtype)

    rel = (rowi - coli).astype(F32)
    pos1 = (rowi + 1).astype(F32)
    left = (n_valid - 1 - rowi).astype(F32)
    ret_out = []
    for h, lg in enumerate(_ret_log_decay()):
        sl = slice(h * RET_DIM, (h + 1) * RET_DIM)
        qh = ret_ref[:, sl]
        kh = ret_ref[:, 256 + h * RET_DIM:256 + (h + 1) * RET_DIM]
        vh = ret_ref[:, 512 + h * RET_DIM:512 + (h + 1) * RET_DIM].astype(BF16)
        gh = ret_ref[:, 768 + h * RET_DIM:768 + (h + 1) * RET_DIM]
        dmat = jnp.where(rel >= 0.0, jnp.exp(lg * jnp.maximum(rel, 0.0)), 0.0)
        attn = _dot_nt(qh.astype(BF16), kh.astype(BF16)) * dmat
        s_h = sr_ref[h]
        o = _dot(attn.astype(BF16), vh) + _dot(qh.astype(BF16), s_h.astype(BF16)) * jnp.exp(lg * pos1)
        sr_ref[h] = math.exp(lg * n_valid) * s_h + _dot_tn((kh * jnp.exp(lg * left)).astype(BF16), vh)
        ret_out.append(_rms(o) * gh)
    o_ref[:, 256:512] = jnp.concatenate(ret_out, axis=1).astype(o_ref.dtype)


def _recurrent(gla, ret, gnorm, sg0, sr0, *, b, t, tt, nbk, n_valid=None):
    nt = t // tt
    n_valid = tt if n_valid is None else n_valid
    assert n_valid == tt or nt == 1
    lvl = jnp.asarray(_gla_level_matrix(tt), BF16)
    tril = jnp.asarray(np.tril(np.ones((tt, tt), np.float32)), BF16)
    full2 = lambda a: pl.BlockSpec(a.shape, lambda bb, i: (0, 0))
    st = lambda a: pl.BlockSpec((nbk,) + a.shape[1:], lambda bb, i: (bb, 0, 0, 0))
    tok = lambda w: pl.BlockSpec((nbk, tt, w), lambda bb, i: (bb, i, 0))
    out, s_gla, s_ret = pl.pallas_call(
        functools.partial(_recur_kernel, tt=tt, n_valid=n_valid),
        out_shape=(jax.ShapeDtypeStruct((b, t, 512), BF16 if tt % 16 == 0 else F32),
                   jax.ShapeDtypeStruct(sg0.shape, F32), jax.ShapeDtypeStruct(sr0.shape, F32)),
        grid=(b // nbk, nt),
        in_specs=[tok(GLA_W), tok(RET_W), full2(gnorm), full2(lvl), full2(tril), st(sg0), st(sr0)],
        out_specs=(tok(512), st(sg0), st(sr0)),
        scratch_shapes=[pltpu.VMEM((nbk,) + sg0.shape[1:], F32), pltpu.VMEM((nbk,) + sr0.shape[1:], F32)],
        compiler_params=_cparams(("parallel", "arbitrary")),
        name="recurrent",
    )(gla.reshape(b, t, GLA_W), ret.reshape(b, t, RET_W), gnorm, lvl, tril, sg0, sr0)
    return out.reshape(b * t, 512), s_gla, s_ret


def _outproj_kernel(x_ref, a_ref, b_ref, w_ref, o_ref):
    ka = a_ref.shape[1]
    o_ref[...] = (x_ref[...] + _dot(a_ref[...].astype(BF16), w_ref[0:ka, :])
                  + _dot(b_ref[...].astype(BF16), w_ref[ka:, :]))


def _outproj(x, a, b_, w, *, tm):
    n = x.shape[0]
    row = lambda a_: pl.BlockSpec((tm, a_.shape[1]), lambda i: (i, 0))
    return pl.pallas_call(
        _outproj_kernel,
        out_shape=jax.ShapeDtypeStruct(x.shape, F32),
        grid=(n // tm,),
        in_specs=[row(x), row(a), row(b_), pl.BlockSpec(w.shape, lambda i: (0, 0))],
        out_specs=row(x),
        compiler_params=_cparams(("parallel",)),
        name="outproj",
    )(x, a, b_, w)


def _cross_kernel(x_ref, gain_ref, wq_ref, wo_ref, kv_ref, o_ref):
    x = x_ref[...]
    xb = (_rms(x) * gain_ref[...]).astype(BF16)
    q = _dot(xb, wq_ref[...])
    heads = []
    for h in range(MEM_HEADS):
        qh = q[:, h * MEM_HD:(h + 1) * MEM_HD].astype(BF16)
        if len(kv_ref.shape) == 4:
            kh = kv_ref[:, 0, h, :].astype(BF16)
            vh = kv_ref[:, 1, h, :].astype(BF16)
        else:
            kh = kv_ref[:, h * MEM_HD:(h + 1) * MEM_HD].astype(BF16)
            vh = kv_ref[:, (MEM_HEADS + h) * MEM_HD:(MEM_HEADS + h + 1) * MEM_HD].astype(BF16)
        s = _dot_nt(qh, kh) * (MEM_HD ** -0.5)
        e = jnp.exp(s - jnp.max(s, axis=-1, keepdims=True))
        p = e / jnp.sum(e, axis=-1, keepdims=True)
        heads.append(_dot(p.astype(BF16), vh))
    att = jnp.concatenate(heads, axis=1).astype(BF16)
    o_ref[...] = x + _dot(att, wo_ref[...])


def _cross(x, gain, wq, wo, memkv, *, b, t, tq, layer=None):
    nq = t // tq
    full = lambda a: pl.BlockSpec(a.shape, lambda bb, i: (0, 0))
    row = pl.BlockSpec((tq, D_MODEL), lambda bb, i: (bb * nq + i, 0))
    if layer is None:
        kv_spec = pl.BlockSpec((memkv.shape[0] // b, memkv.shape[1]), lambda bb, i: (bb, 0))
    else:
        kv_spec = pl.BlockSpec((None, None) + memkv.shape[2:], lambda bb, i: (layer, bb, 0, 0, 0, 0))
    return pl.pallas_call(
        _cross_kernel,
        out_shape=jax.ShapeDtypeStruct(x.shape, F32),
        grid=(b, nq),
        in_specs=[row, full(gain), full(wq), full(wo), kv_spec],
        out_specs=row,
        compiler_params=_cparams(("parallel", "arbitrary")),
        name="cross",
    )(x, gain, wq, wo, memkv)


def _ffn_kernel(x_ref, gain_ref, w1_ref, w2_ref, gf_ref, o_ref, xn_ref, acc_ref, *, final_norm):
    j = pl.program_id(1)

    @pl.when(j == 0)
    def _():
        xn_ref[...] = (_rms(x_ref[...]) * gain_ref[...]).astype(BF16)
        acc_ref[...] = x_ref[...]

    h = jnp.maximum(_dot(xn_ref[...], w1_ref[...]), 0.0)
    acc_ref[...] += _dot((h * h).astype(BF16), w2_ref[...])

    @pl.when(j == pl.num_programs(1) - 1)
    def _():
        y = acc_ref[...]
        if final_norm:
            y = _rms(y) * gf_ref[...]
        o_ref[...] = y


def _ffn(x, gain, w1, w2, gfinal, *, tm, tf, final_norm):
    n = x.shape[0]
    row = pl.BlockSpec((tm, D_MODEL), lambda i, j: (i, 0))
    vec = pl.BlockSpec((1, D_MODEL), lambda i, j: (0, 0))
    return pl.pallas_call(
        functools.partial(_ffn_kernel, final_norm=final_norm),
        out_shape=jax.ShapeDtypeStruct(x.shape, F32),
        grid=(n // tm, D_FF // tf),
        in_specs=[row, vec, pl.BlockSpec((D_MODEL, tf), lambda i, j: (0, j)),
                  pl.BlockSpec((tf, D_MODEL), lambda i, j: (j, 0)), vec],
        out_specs=row,
        scratch_shapes=[pltpu.VMEM((tm, D_MODEL), BF16), pltpu.VMEM((tm, D_MODEL), F32)],
        compiler_params=_cparams(("parallel", "arbitrary")),
        name="ffn",
    )(x, gain, w1, w2, gfinal)


def _matmul_kernel(x_ref, w_ref, o_ref):
    o_ref[...] = _dot(x_ref[...].astype(BF16), w_ref[...])


def _matmul(x, w, *, tm):
    n, k = x.shape
    return pl.pallas_call(
        _matmul_kernel,
        out_shape=jax.ShapeDtypeStruct((n, w.shape[1]), F32),
        grid=(n // tm,),
        in_specs=[pl.BlockSpec((tm, k), lambda i: (i, 0)), pl.BlockSpec(w.shape, lambda i: (0, 0))],
        out_specs=pl.BlockSpec((tm, w.shape[1]), lambda i: (i, 0)),
        compiler_params=_cparams(("parallel",)),
        name="memkv",
    )(x, w)


TS = 8


def _pool_slabs(pool):
    depth, n_pool = pool.shape[:2]
    return jnp.transpose(pool, (0, 1, 3, 4, 5, 2)).reshape(depth * n_pool, KV_W, pool.shape[2])


def _cmp_pages_kernel(pt_ref, pool_ref, wexp_hbm, pe_ref, w1_ref, w2_ref, o_ref, buf_ref, wbuf_ref, sem_ref,
                      *, pg, base):
    s = pl.program_id(0)
    ns = pl.num_programs(0)
    slab = pool_ref.shape[1]
    pitch = buf_ref.shape[1] // pg
    n_pairs = wbuf_ref.shape[1]
    blocks_per_page = PAGE_SIZE // CMP_BLOCK

    def page_copy(page, slot, j):
        return pltpu.make_async_copy(pool_ref.at[page], buf_ref.at[slot, pl.ds(j * pitch, slab), :], sem_ref.at[slot])

    def fetch(step, slot):
        def body(j, c):
            page_copy(base + pt_ref[step * pg + j], slot, j).start()
            return c
        lax.fori_loop(0, pg, body, 0)

    @pl.when(s == 0)
    def _():
        weights = pltpu.make_async_copy(wexp_hbm, wbuf_ref, sem_ref.at[2])
        weights.start()
        fetch(0, 0)
        weights.wait()

    @pl.when(s + 1 < ns)
    def _():
        fetch(s + 1, lax.rem(s + 1, 2))

    slot = lax.rem(s, 2)

    def wait_body(j, c):
        page_copy(0, slot, j).wait()
        return c
    lax.fori_loop(0, pg, wait_body, 0)

    for kv in range(2):
        pew = sum(_dot(part, w1_ref[kv]) for part in _split3(pe_ref[kv]))[0:1]
        pew = jnp.concatenate([pew] * blocks_per_page, axis=1)
        for g in range(NSA_KV_HEADS):
            kvg = kv * NSA_KV_HEADS + g

            def body(dd, acc):
                r0 = kvg * HEAD_DIM + 2 * dd
                x0 = buf_ref[slot, pl.ds(r0, pg, stride=pitch), :]
                x1 = buf_ref[slot, pl.ds(r0 + 1, pg, stride=pitch), :]
                x = jnp.concatenate([x0, x1], axis=1).astype(BF16)
                return acc + _dot(x, wbuf_ref[kv, dd])
            acc = lax.fori_loop(0, n_pairs, body, jnp.zeros((pg, w2_ref.shape[1]), F32), unroll=8)
            width = w2_ref.shape[2]
            o_ref[:, kvg * width:(kvg + 1) * width] = _dot(_gelu_tanh(acc + pew).astype(BF16), w2_ref[kv])


def _pack_compress_pages(pe, w1, w2):
    d = HEAD_DIM
    nblk = PAGE_SIZE // CMP_BLOCK
    hid = w1.shape[-1]
    eye = jnp.eye(nblk, dtype=w1.dtype)
    w1r = w1.reshape(2, CMP_BLOCK, d, hid)
    wexp = jnp.einsum('kjdc,nm->kdnjmc', w1r, eye).reshape(2, d // 2, 2 * PAGE_SIZE, nblk * hid)
    w2bd = jnp.einsum('kce,nm->kncme', w2, eye).reshape(2, nblk * hid, nblk * w2.shape[-1])
    pe_rows = jnp.pad(pe.reshape(2, 1, CMP_BLOCK * d), ((0, 0), (0, 7), (0, 0)))
    return wexp.astype(BF16), pe_rows, w1.astype(BF16), w2bd.astype(BF16)


def _compress_pages(page_table_flat, pool_slabs, wexp, pe_rows, w1b, w2bd, *, base, pg):
    n_pages = page_table_flat.shape[0]
    slab = pool_slabs.shape[1]
    width = w2bd.shape[2]
    full3 = lambda a: pl.BlockSpec(a.shape, lambda i, pt: (0, 0, 0))
    return pl.pallas_call(
        functools.partial(_cmp_pages_kernel, pg=pg, base=base),
        out_shape=jax.ShapeDtypeStruct((n_pages, 2 * NSA_KV_HEADS * width), F32),
        grid_spec=pltpu.PrefetchScalarGridSpec(
            num_scalar_prefetch=1,
            grid=(n_pages // pg,),
            in_specs=[pl.BlockSpec(memory_space=pl.ANY), pl.BlockSpec(memory_space=pl.ANY),
                      full3(pe_rows), full3(w1b), full3(w2bd)],
            out_specs=pl.BlockSpec((pg, 2 * NSA_KV_HEADS * width), lambda i, pt: (i, 0)),
            scratch_shapes=[pltpu.VMEM((2, pg * (slab + 8), pool_slabs.shape[2]), F32), pltpu.VMEM(wexp.shape, BF16),
                            pltpu.SemaphoreType.DMA((3,))]),
        compiler_params=_cparams(("arbitrary",)),
        name="compress_pages",
    )(page_table_flat, pool_slabs, wexp, pe_rows, w1b, w2bd)


def _heads_to_lanes(o, ts):
    return jnp.concatenate([o[r * ts:(r + 1) * ts] for r in range(NSA_REP)], axis=1)


def _nsa_sample_a_kernel(q_ref, kcp_ref, kcn_ref, wc_ref, wn_ref, ocmp_ref, owin_ref, idx_ref, wout_ref,
                         *, past, n_new_blk, tv):
    ts = q_ref.shape[0]
    q = q_ref[...].astype(F32)
    hp = kcp_ref.shape[0]
    bpp = PAGE_SIZE // CMP_BLOCK
    plane_w = bpp * HEAD_DIM
    nbp = hp * bpp
    nsb_past = nbp * CMP_BLOCK // SEL_BLOCK
    nsb = nsb_past + (n_new_blk + 1) // 2
    wb = wc_ref.shape[1]
    m_rows = NSA_REP * ts
    row = lax.broadcasted_iota(jnp.int32, (m_rows, 1), 0)
    tpos = past + lax.rem(row, ts)
    t8 = past + lax.broadcasted_iota(jnp.int32, (ts, 1), 0)
    k_new = kcn_ref[...]
    w_old = wc_ref[...]
    w_new = wn_ref[...]
    nn = k_new.shape[0]
    pagei = lax.broadcasted_iota(jnp.int32, (1, hp), 1)
    cn = lax.broadcasted_iota(jnp.int32, (1, nn), 1)
    mask_p = jnp.concatenate([(bpp * pagei + n + 1) * CMP_BLOCK - 1 <= tpos for n in range(bpp)], axis=1)
    mask_n = ((nbp + cn + 1) * CMP_BLOCK - 1 <= tpos) & (cn < n_new_blk)
    kp_old = past - wb + lax.broadcasted_iota(jnp.int32, (1, wb), 1)
    kp_new = past + lax.broadcasted_iota(jnp.int32, (1, w_new.shape[0]), 1)
    wmask_old = (kp_old <= tpos) & (kp_old > tpos - WINDOW) & (kp_old >= 0)
    wmask_new = (kp_new <= tpos) & (kp_new > tpos - WINDOW)
    lane = lax.broadcasted_iota(jnp.int32, (1, LANES), 1)
    j = jnp.concatenate([2 * pagei, 2 * pagei + 1, nsb_past + lane], axis=1)
    width = 2 * hp + LANES

    new_t = jnp.concatenate([w_new, jnp.zeros((LANES - w_new.shape[0], w_new.shape[1]), F32)], axis=0).T
    wout_ref[...] = jnp.concatenate([w_old[:, tv:], new_t[:, 0:tv]], axis=1)

    def joint_softmax(parts):
        mx = None
        for s, mk in parts:
            cur = jnp.max(jnp.where(mk, s, NEG), axis=-1, keepdims=True)
            mx = cur if mx is None else jnp.maximum(mx, cur)
        mx = jnp.where(mx > 0.5 * NEG, mx, 0.0)
        es = [jnp.where(mk, jnp.exp(jnp.where(mk, s, NEG) - mx), 0.0) for s, mk in parts]
        den = sum(jnp.sum(e, axis=-1, keepdims=True) for e in es)
        inv = 1.0 / jnp.where(den > 0.0, den, 1.0)
        return [e * inv for e in es]

    def fold_heads(p):
        out = p[0:ts]
        for r in range(1, NSA_REP):
            out = out + p[r * ts:(r + 1) * ts]
        return out

    for g in range(NSA_KV_HEADS):
        qg = _stack_heads(q, g).astype(BF16)
        kcol = slice(g * HEAD_DIM, (g + 1) * HEAD_DIM)
        vcol = slice((NSA_KV_HEADS + g) * HEAD_DIM, (NSA_KV_HEADS + g + 1) * HEAD_DIM)

        kplanes = [kcp_ref[:, g * plane_w + n * HEAD_DIM:g * plane_w + (n + 1) * HEAD_DIM].astype(BF16)
                   for n in range(bpp)]
        vplanes = [kcp_ref[:, (NSA_KV_HEADS + g) * plane_w + n * HEAD_DIM:
                           (NSA_KV_HEADS + g) * plane_w + (n + 1) * HEAD_DIM].astype(BF16) for n in range(bpp)]
        s_past = jnp.concatenate([_dot_nt(qg, kp) for kp in kplanes], axis=1)
        p_p, p_n = joint_softmax([(s_past, mask_p), (_dot_nt(qg, k_new[:, kcol].astype(BF16)), mask_n)])
        o_cmp = _dot(p_n.astype(BF16), k_new[:, vcol].astype(BF16))
        for n in range(bpp):
            o_cmp = o_cmp + _dot(p_p[:, n * hp:(n + 1) * hp].astype(BF16), vplanes[n])
        ocmp_ref[:, g * NSA_REP * HEAD_DIM:(g + 1) * NSA_REP * HEAD_DIM] = _heads_to_lanes(o_cmp, ts)

        pp = fold_heads(p_p)
        pn = fold_heads(p_n)
        imp_new = jnp.zeros((ts, LANES), F32)
        for c in range(n_new_blk):
            imp_new = imp_new + jnp.where(lane == c // 2, pn[:, c:c + 1], 0.0)
        imp = jnp.concatenate([pp[:, 0:hp] + pp[:, hp:2 * hp], pp[:, 2 * hp:3 * hp] + pp[:, 3 * hp:4 * hp],
                               imp_new], axis=1)
        cur = t8 // SEL_BLOCK
        forced = (j == 0) | (j == cur) | (j == cur - 1)
        valid = j * SEL_BLOCK <= t8
        score = jnp.where(forced, -NEG, jnp.where(valid, imp, NEG))
        alive = j < nsb
        chosen = jnp.zeros((ts, LANES), jnp.int32)
        for n in range(min(N_SELECT, nsb)):
            best = jnp.max(jnp.where(alive, score, 2.0 * NEG), axis=-1, keepdims=True)
            pick = jnp.min(jnp.where(alive & (score == best), j, nsb), axis=-1, keepdims=True)
            alive = alive & (j != pick)
            chosen = jnp.where(lane == n, pick, chosen)
        idx_ref[g] = chosen

        kt_old = w_old[g * HEAD_DIM:(g + 1) * HEAD_DIM, :].astype(BF16)
        vt_old = w_old[(NSA_KV_HEADS + g) * HEAD_DIM:(NSA_KV_HEADS + g + 1) * HEAD_DIM, :].astype(BF16)
        pw_old, pw_new = joint_softmax([(_dot(qg, kt_old), wmask_old),
                                        (_dot_nt(qg, w_new[:, kcol].astype(BF16)), wmask_new)])
        o_win = _dot_nt(pw_old.astype(BF16), vt_old) + _dot(pw_new.astype(BF16), w_new[:, vcol].astype(BF16))
        owin_ref[:, g * NSA_REP * HEAD_DIM:(g + 1) * NSA_REP * HEAD_DIM] = _heads_to_lanes(o_win, ts)


def _nsa_sample_a(q, kc_past, kc_new, win_old, win_new, *, b, past, n_new_blk, tv):
    hp = kc_past.shape[0] // b
    blk3 = lambda a: pl.BlockSpec((None,) + a.shape[1:], lambda bb: (bb, 0, 0))
    return pl.pallas_call(
        functools.partial(_nsa_sample_a_kernel, past=past, n_new_blk=n_new_blk, tv=tv),
        out_shape=(jax.ShapeDtypeStruct((b, TS, 512), F32), jax.ShapeDtypeStruct((b, TS, 512), F32),
                   jax.ShapeDtypeStruct((b, NSA_KV_HEADS, TS, LANES), jnp.int32),
                   jax.ShapeDtypeStruct(win_old.shape, F32)),
        grid=(b,),
        in_specs=[blk3(q), pl.BlockSpec((hp, kc_past.shape[1]), lambda bb: (bb, 0)), blk3(kc_new), blk3(win_old),
                  blk3(win_new)],
        out_specs=(pl.BlockSpec((None, TS, 512), lambda bb: (bb, 0, 0)),
                   pl.BlockSpec((None, TS, 512), lambda bb: (bb, 0, 0)),
                   pl.BlockSpec((None, NSA_KV_HEADS, TS, LANES), lambda bb: (bb, 0, 0, 0)),
                   blk3(win_old)),
        compiler_params=_cparams(("parallel",)),
        name="nsa_sample_a",
    )(q, kc_past, kc_new, win_old, win_new)


def _nsa_sel_kernel(idx_ref, pt_ref, pool_ref, new_ref, q_ref, ocmp_ref, owin_ref, gate_ref, o_ref,
                    buf_ref, sem_ref, *, past, base, n_pages, tv, nsel):
    bb = pl.program_id(0)
    nb = pl.num_programs(0)
    ts = q_ref.shape[0]
    page = pool_ref.shape[2]
    blocks_per_page = page // SEL_BLOCK
    nsb_past = past // SEL_BLOCK
    kv_rows = NSA_KV_HEADS * HEAD_DIM

    def block_id(b_, g, t, n):
        return idx_ref[((b_ * NSA_KV_HEADS + g) * ts + t) * nsel + n]

    def copies(src_slab, slot, g, t, n):
        lanes = pl.ds(pl.multiple_of(n * page, page), page)
        return [pltpu.make_async_copy(src_slab.at[pl.ds(kv * kv_rows + g * HEAD_DIM, HEAD_DIM), :],
                                      buf_ref.at[slot, kv, g * tv + t, :, lanes], sem_ref.at[slot])
                for kv in range(2)]

    def fetch(b_, slot):
        for g in range(NSA_KV_HEADS):
            for t in range(tv):
                def body(n, c):
                    bid = block_id(b_, g, t, n)

                    @pl.when(bid < nsb_past)
                    def _():
                        src = pool_ref.at[base + pt_ref[b_ * n_pages + bid // blocks_per_page]]
                        for cp in copies(src, slot, g, t, n):
                            cp.start()

                    @pl.when(bid >= nsb_past)
                    def _():
                        for cp in copies(new_ref.at[b_], slot, g, t, n):
                            cp.start()
                    return c
                lax.fori_loop(0, nsel, body, 0)

    @pl.when(bb == 0)
    def _():
        fetch(0, 0)

    @pl.when(bb + 1 < nb)
    def _():
        fetch(bb + 1, lax.rem(bb + 1, 2))

    slot = lax.rem(bb, 2)

    def wait_body(n, c):
        for cp in copies(pool_ref.at[0], slot, 0, 0, 0):
            cp.wait()
        return c
    lax.fori_loop(0, NSA_KV_HEADS * tv * nsel, wait_body, 0)

    q = q_ref[...].astype(F32)
    gates = gate_ref[...]
    o_cmp = ocmp_ref[...]
    o_win = owin_ref[...]
    m_rows = NSA_REP * ts
    row = lax.broadcasted_iota(jnp.int32, (m_rows, 1), 0)
    trow = lax.rem(row, ts)
    tpos = past + trow
    nk = nsel * page
    lane = lax.broadcasted_iota(jnp.int32, (1, nk), 1)
    in_page = lax.rem(lane, page)
    outs = []
    for g in range(NSA_KV_HEADS):
        qg = _stack_heads(q, g).astype(BF16)
        o_sel = jnp.zeros((m_rows, HEAD_DIM), F32)
        for t in range(tv):
            kt = buf_ref[slot, 0, g * tv + t].astype(BF16)
            vt = buf_ref[slot, 1, g * tv + t].astype(BF16)
            first_pos = jnp.zeros((1, nk), jnp.int32)
            half = jnp.zeros((1, nk), jnp.int32)
            for n in range(nsel):
                bid = block_id(bb, g, t, n)
                here = lane // page == n
                first_pos = jnp.where(here, (bid // blocks_per_page) * page, first_pos)
                half = jnp.where(here, lax.rem(bid, blocks_per_page), half)
            kpos = first_pos + in_page
            mask = (in_page // SEL_BLOCK == half) & (kpos <= tpos)
            s = jnp.where(mask, _dot(qg, kt), NEG)
            e = jnp.where(mask, jnp.exp(s - jnp.max(s, axis=-1, keepdims=True)), 0.0)
            p = e / jnp.sum(e, axis=-1, keepdims=True)
            o_sel = o_sel + jnp.where(trow == t, _dot_nt(p.astype(BF16), vt), 0.0)
        for r in range(NSA_REP):
            h = g * NSA_REP + r
            hs = slice(h * HEAD_DIM, (h + 1) * HEAD_DIM)
            outs.append(o_cmp[:, hs] * gates[:, 3 * h:3 * h + 1] + o_sel[r * ts:(r + 1) * ts] * gates[:, 3 * h + 1:3 * h + 2]
                        + o_win[:, hs] * gates[:, 3 * h + 2:3 * h + 3])
    o_ref[...] = jnp.concatenate(outs, axis=1)


def _nsa_sample_sel(idx_flat, page_table_flat, pool_slabs, new_slabs, q, o_cmp, o_win, gates, *, b, past, base, tv):
    nsel = idx_flat.shape[0] // (b * NSA_KV_HEADS * TS)
    page = pool_slabs.shape[2]
    blk3 = lambda a: pl.BlockSpec((None,) + a.shape[1:], lambda bb, i_, p_: (bb, 0, 0))
    return pl.pallas_call(
        functools.partial(_nsa_sel_kernel, past=past, base=base, n_pages=page_table_flat.shape[0] // b,
                          tv=tv, nsel=nsel),
        out_shape=jax.ShapeDtypeStruct((b, TS, 512), F32),
        grid_spec=pltpu.PrefetchScalarGridSpec(
            num_scalar_prefetch=2,
            grid=(b,),
            in_specs=[pl.BlockSpec(memory_space=pl.ANY), pl.BlockSpec(memory_space=pl.ANY),
                      blk3(q), blk3(o_cmp), blk3(o_win), blk3(gates)],
            out_specs=pl.BlockSpec((None, TS, 512), lambda bb, i_, p_: (bb, 0, 0)),
            scratch_shapes=[pltpu.VMEM((2, 2, NSA_KV_HEADS * tv, HEAD_DIM, nsel * page), F32),
                            pltpu.SemaphoreType.DMA((2,))]),
        compiler_params=_cparams(("arbitrary",)),
        name="nsa_sample_sel",
    )(idx_flat, page_table_flat, pool_slabs, new_slabs, q, o_cmp, o_win, gates)


def _layer_weights(l, norm_mix, w_in, gla_wa2, gla_ba, gla_norm, cmp_pe, cmp_w1, cmp_w2, w_out, norm_mem,
                   w_mq, w_mkv, w_mo, norm_ffn, w_ff1, w_ff2):
    pe_big, w1_big, w2_big = _pack_compress(cmp_pe[l], cmp_w1[l], cmp_w2[l])
    return dict(
        pages=_pack_compress_pages(cmp_pe[l], cmp_w1[l], cmp_w2[l]), norm_mix=norm_mix[l][None, :], w_in=_pack_w_in(w_in[l]), wa2=_pack_wa2(gla_wa2[l]),
        ba=gla_ba[l][None, :], gnorm=gla_norm[l][None, :], pe_big=pe_big, w1_big=w1_big, w2_big=w2_big,
        w_out=w_out[l].astype(BF16), norm_mem=norm_mem[l][None, :], w_mq=w_mq[l].astype(BF16),
        w_mkv=w_mkv[l].astype(BF16), w_mo=w_mo[l].astype(BF16), norm_ffn=norm_ffn[l][None, :],
        w_ff1=w_ff1[l].astype(BF16), w_ff2=w_ff2[l].astype(BF16))


def _even_odd(kcv, b):
    nb = kcv.shape[0] // b
    return kcv.reshape(b, nb // 2, 2, -1).transpose(0, 2, 1, 3).reshape(b * nb, -1)


def _prompt_layer(x, mem, p, tabs, gfinal, *, b, t, final_norm):
    q, cmp_rows, slc_rows, win_rows, gates, gla, ret = _project(
        x, p['norm_mix'], p['w_in'], p['wa2'], p['ba'], tabs, tm=512)
    kcv = _compress(cmp_rows.reshape(-1, CMP_BLOCK * KV_W), p['pe_big'], p['w1_big'], p['w2_big'], tr=128)
    o_nsa = _nsa_prompt(q, _even_odd(kcv, b), slc_rows, win_rows, gates, b=b, t=t, tq=256, tk=256)
    sg0 = jnp.zeros((b, GLA_HEADS, GLA_DK, GLA_DV), F32)
    sr0 = jnp.zeros((b, RET_HEADS, RET_DIM, RET_DIM), F32)
    o_rec, s_gla, s_ret = _recurrent(gla, ret, p['gnorm'], sg0, sr0, b=b, t=t, tt=128, nbk=4 if b % 4 == 0 else 1)
    x = _outproj(x, o_nsa, o_rec, p['w_out'], tm=512)
    mem_kv = _matmul(mem, p['w_mkv'], tm=256)
    x = _cross(x, p['norm_mem'], p['w_mq'], p['w_mo'], mem_kv, b=b, t=t, tq=512)
    x = _ffn(x, p['norm_ffn'], p['w_ff1'], p['w_ff2'], gfinal, tm=1024, tf=1024, final_norm=final_norm)
    wlen = min(WINDOW, t)
    win_tail = win_rows.reshape(b, t, KV_W)[:, t - wlen:]
    return x, (cmp_rows, slc_rows, win_tail, s_gla, s_ret, mem_kv)


def _sample_layer(x, p, tabs, gfinal, cmp_slabs, slc_slabs, win_slabs, sg0, sr0, mem_cache, pt_flat,
                  *, b, past, tv, base, layer, final_norm):
    n = b * TS
    q, cmp_rows, slc_rows, win_rows, gates, gla, ret = _project(
        x, p['norm_mix'], p['w_in'], p['wa2'], p['ba'], tabs, tm=n, valid_period=(TS, tv))
    rows3 = lambda a: a.reshape(b, TS, a.shape[-1])
    pad_blk = lambda a: jnp.pad(rows3(a), ((0, 0), (0, SEL_BLOCK - TS), (0, 0)))
    n_new_blk = SEL_BLOCK // CMP_BLOCK
    pg = min(128, past // PAGE_SIZE)
    kc_past = _compress_pages(pt_flat, cmp_slabs, *p['pages'], base=base, pg=pg)
    kc_new = _compress(pad_blk(cmp_rows).reshape(b * n_new_blk, CMP_BLOCK * KV_W), p['pe_big'], p['w1_big'],
                       p['w2_big'], tr=b * n_new_blk)
    kc_new = jnp.pad(kc_new.reshape(b, n_new_blk, KV_W), ((0, 0), (0, TS - n_new_blk), (0, 0)))
    q3 = rows3(q)
    o_cmp, o_win, idx, new_win = _nsa_sample_a(q3, kc_past, kc_new, win_slabs, rows3(win_rows), b=b, past=past,
                                               n_new_blk=n_new_blk, tv=tv)
    nsel = min(N_SELECT, past // SEL_BLOCK + 1)
    slc_new = jnp.pad(jnp.swapaxes(rows3(slc_rows), 1, 2), ((0, 0), (0, 0), (0, cmp_slabs.shape[2] - TS)))
    o_nsa = _nsa_sample_sel(idx[..., :nsel].reshape(-1), pt_flat, slc_slabs, slc_new, q3, o_cmp, o_win,
                            rows3(gates), b=b, past=past, base=base, tv=tv)
    o_rec, s_gla, s_ret = _recurrent(gla, ret, p['gnorm'], sg0, sr0, b=b, t=TS, tt=TS, nbk=2 if b % 2 == 0 else 1, n_valid=tv)
    x = _outproj(x, o_nsa.reshape(n, -1), o_rec, p['w_out'], tm=n)
    x = _cross(x, p['norm_mem'], p['w_mq'], p['w_mo'], mem_cache, b=b, t=TS, tq=TS, layer=layer)
    x = _ffn(x, p['norm_ffn'], p['w_ff1'], p['w_ff2'], gfinal, tm=n, tf=1024, final_norm=final_norm)
    return x, (rows3(cmp_rows)[:, :tv], rows3(slc_rows)[:, :tv], new_win, s_gla, s_ret)


def kernel(x_prompt, x_sample, mem_prompt, cache_cmp_kv, cache_slc_kv, cache_win_kv, state_gla, state_ret,
           cache_mem_kv, page_table, norm_mix, w_in, gla_wa2, gla_ba, gla_norm, cmp_pe, cmp_w1, cmp_w2, w_out,
           norm_mem, w_mq, w_mkv, w_mo, norm_ffn, w_ff1, w_ff2, norm_final):
    depth = w_in.shape[0]
    b, t, d = x_prompt.shape
    g, hd = NSA_KV_HEADS, HEAD_DIM
    gfinal = norm_final[None, :]
    tabs_p = _all_rope_tables(jnp.arange(t))
    hp = x_prompt.reshape(b * t, d)
    mem = mem_prompt.reshape(-1, d)
    sb, tv = x_sample.shape[:2]
    assert tv <= TS
    past = page_table.shape[1] * PAGE_SIZE
    tabs_s = _all_rope_tables(past + jnp.arange(sb * TS) % TS)
    hs = jnp.pad(x_sample, ((0, 0), (0, TS - tv), (0, 0))).reshape(sb * TS, d)
    pt_flat = page_table.reshape(-1)
    n_pool = cache_cmp_kv.shape[1]
    cmp_slabs = _pool_slabs(cache_cmp_kv)
    slc_slabs = _pool_slabs(cache_slc_kv)
    win_slabs = jnp.transpose(cache_win_kv, (0, 1, 3, 4, 5, 2)).reshape(depth, sb, KV_W, -1)
    new_p = [[] for _ in range(6)]
    new_s = [[] for _ in range(5)]
    for l in range(depth):
        p = _layer_weights(l, norm_mix, w_in, gla_wa2, gla_ba, gla_norm, cmp_pe, cmp_w1, cmp_w2, w_out, norm_mem,
                           w_mq, w_mkv, w_mo, norm_ffn, w_ff1, w_ff2)
        last = l == depth - 1
        hp, st = _prompt_layer(hp, mem, p, tabs_p, gfinal, b=b, t=t, final_norm=last)
        for lst, a in zip(new_p, st):
            lst.append(a)
        hs, st = _sample_layer(hs, p, tabs_s, gfinal, cmp_slabs, slc_slabs, win_slabs[l], state_gla[l], state_ret[l],
                               cache_mem_kv, pt_flat, b=sb, past=past, tv=tv, base=l * n_pool, layer=l,
                               final_norm=last)
        for lst, a in zip(new_s, st):
            lst.append(a)
    cmp_p, slc_p, win_p, gla_p, ret_p, mem_p = [jnp.stack(a) for a in new_p]
    cmp_s, slc_s, win_s, gla_s, ret_s = [jnp.stack(a) for a in new_s]
    n_mem = mem_prompt.shape[1]
    kv6 = lambda a: a.reshape(a.shape[:3] + (2, g, hd))
    win_s = jnp.transpose(win_s.reshape(depth, sb, 2, g, hd, -1), (0, 1, 5, 2, 3, 4))
    return (hp.reshape(b, t, d), hs.reshape(sb, TS, d)[:, :tv],
            kv6(cmp_p.reshape(depth, b, t, KV_W)), kv6(slc_p.reshape(depth, b, t, KV_W)), kv6(win_p), gla_p, ret_p,
            mem_p.reshape(depth, b, n_mem, 2, MEM_HEADS, MEM_HD),
            kv6(cmp_s), kv6(slc_s), win_s, gla_s, ret_s)
```
